```python
import math
import jax
import jax.numpy as jnp
from jax import lax
import numpy as np

D_MODEL = 1024
BATCH = 16
SEQ = 4096
DEPTH = 1

ATT_HEADS = 16
ATT_KV_HEADS = 4
ATT_HEAD_DIM = 64
ATT_W = ATT_HEADS * ATT_HEAD_DIM
KV_W = ATT_KV_HEADS * ATT_HEAD_DIM
WINDOW = 128
BLOCK = 128
ROPE_THETA = 10000.0

SSD_HEADS = 8
SSD_HEAD_DIM = 64
SSD_W = SSD_HEADS * SSD_HEAD_DIM
SSD_GROUPS = 2
SSD_HPG = SSD_HEADS // SSD_GROUPS
SSD_STATE = 128
SSD_CONV = 5
CHUNK = 128
N_DIR = 2
XBC_W = SSD_W + 2 * SSD_GROUPS * SSD_STATE

MEM_LEN = 256
MEM_HEADS = 4
MEM_HEAD_DIM = 128
MEM_W = MEM_HEADS * MEM_HEAD_DIM

MIX_W = ATT_W + SSD_W + MEM_W
SPLIT_SIZES = (ATT_W, KV_W, KV_W, ATT_W, SSD_W, XBC_W, N_DIR * SSD_HEADS, MEM_W, MEM_W)
IN_W = ATT_W + 2 * KV_W + ATT_W + SSD_W + XBC_W + N_DIR * SSD_HEADS + 2 * MEM_W
EPS = 1e-6

kernel_name = "hymba_style_bidir_swa_ssd_memxattn_layer"


def _split_points():
    pts, acc = [], 0
    for s in SPLIT_SIZES[:-1]:
        acc += s
        pts.append(acc)
    return pts


def rms_norm(x, w):
    xf = x.astype(jnp.float32)
    y = xf * lax.rsqrt(jnp.mean(xf * xf, axis=-1, keepdims=True) + EPS)
    return (y * w.astype(jnp.float32)).astype(x.dtype)


def gated_group_rmsnorm(y, z, w):
    g = (y * jax.nn.silu(z)).astype(jnp.float32)
    b, l, c = g.shape
    g = g.reshape(b, l, SSD_GROUPS, c // SSD_GROUPS)
    g = g * lax.rsqrt(jnp.mean(g * g, axis=-1, keepdims=True) + EPS)
    return (g.reshape(b, l, c) * w.astype(jnp.float32)).astype(y.dtype)


def rotary(t, cos, sin):
    t1, t2 = jnp.split(t, 2, axis=-1)
    c = cos[None, :, None, :].astype(t.dtype)
    s = sin[None, :, None, :].astype(t.dtype)
    return jnp.concatenate([t1 * c - t2 * s, t1 * s + t2 * c], axis=-1)


def window_attention(q, k, v, sink):
    b, l, h, dh = q.shape
    kvh = k.shape[2]
    r = h // kvh
    nb = l // BLOCK
    span = BLOCK + 2 * WINDOW
    pad = ((0, 0), (WINDOW, WINDOW), (0, 0), (0, 0))
    kp = jnp.pad(k, pad)
    vp = jnp.pad(v, pad)
    qg = q.reshape(b, l, kvh, r, dh)
    scale = dh ** -0.5
    sink_f = sink.astype(jnp.float32).reshape(1, kvh, r, 1, 1)

    def one_block(i):
        start = i * BLOCK
        qb = lax.dynamic_slice_in_dim(qg, start, BLOCK, axis=1)
        kb = lax.dynamic_slice_in_dim(kp, start, span, axis=1)
        vb = lax.dynamic_slice_in_dim(vp, start, span, axis=1)
        s = jnp.einsum('bqkrd,bskd->bkrqs', qb, kb).astype(jnp.float32) * scale
        qpos = start + jnp.arange(BLOCK)
        kpos = start - WINDOW + jnp.arange(span)
        valid = ((jnp.abs(qpos[:, None] - kpos[None, :]) <= WINDOW)
                 & (kpos >= 0)[None, :] & (kpos < l)[None, :])
        s = jnp.where(valid, s, -jnp.inf)
        sink_col = jnp.broadcast_to(sink_f, s.shape[:-1] + (1,))
        p = jax.nn.softmax(jnp.concatenate([s, sink_col], axis=-1), axis=-1)[..., :-1]
        o = jnp.einsum('bkrqs,bskd->bqkrd', p.astype(vb.dtype), vb)
        return o.reshape(b, BLOCK, h, dh)

    out = lax.map(one_block, jnp.arange(nb))
    return out.transpose(1, 0, 2, 3, 4).reshape(b, l, h, dh)


def centred_depthwise_conv(u, w, bias):
    k = w.shape[0]
    y = lax.conv_general_dilated(
        u, w[:, None, :].astype(u.dtype), window_strides=(1,),
        padding=[((k - 1) // 2, k // 2)],
        dimension_numbers=('NWC', 'WIO', 'NWC'),
        feature_group_count=u.shape[-1])
    return y + bias.astype(u.dtype)


def ssd_scan(x, dt, a, bm, cm):
    bsz, l, g, r, p = x.shape
    n = bm.shape[-1]
    nc = l // CHUNK
    xdt = (x * dt[..., None]).reshape(bsz, nc, CHUNK, g, r, p)
    adt = (dt.astype(jnp.float32) * a.astype(jnp.float32)).reshape(bsz, nc, CHUNK, g, r)
    bc = bm.reshape(bsz, nc, CHUNK, g, n)
    cc = cm.reshape(bsz, nc, CHUNK, g, n)
    a_cs = jnp.cumsum(adt, axis=2).transpose(0, 1, 3, 4, 2)
    seg = a_cs[..., :, None] - a_cs[..., None, :]
    lower = jnp.tril(jnp.ones((CHUNK, CHUNK), dtype=bool))
    lmat = jnp.exp(jnp.where(lower, seg, -jnp.inf)).astype(x.dtype)
    cb = jnp.einsum('bclgn,bcsgn->bcgls', cc, bc)
    y_diag = jnp.einsum('bcgls,bcgrls,bcsgrp->bclgrp', cb, lmat, xdt)
    decay_in = jnp.exp(a_cs[..., -1:] - a_cs).astype(x.dtype)
    states = jnp.einsum('bcsgn,bcgrs,bcsgrp->bcgrpn', bc, decay_in, xdt)
    chunk_decay = jnp.exp(a_cs[..., -1]).astype(x.dtype)

    def step(hstate, inp):
        st, dec = inp
        return hstate * dec[..., None, None] + st, hstate

    h0 = jnp.zeros((bsz, g, r, p, n), dtype=states.dtype)
    _, prev = lax.scan(step, h0, (states.transpose(1, 0, 2, 3, 4, 5),
                                  chunk_decay.transpose(1, 0, 2, 3)))
    prev = prev.transpose(1, 0, 2, 3, 4, 5)
    y_off = jnp.einsum('bclgn,bcgrpn,bcgrl->bclgrp', cc, prev,
                       jnp.exp(a_cs).astype(x.dtype))
    return (y_diag + y_off).reshape(bsz, l, g, r, p)


def setup_inputs(seed: int = 0) -> dict:
    key = jax.random.key(seed)
    ks = jax.random.split(key, 20)
    f32 = jnp.float32
    x = jax.random.normal(ks[0], (BATCH, SEQ, D_MODEL), f32)
    mem = jax.random.normal(ks[1], (BATCH, MEM_LEN, D_MODEL), f32)
    norm_mem_w = 1.0 + 0.02 * jax.random.normal(ks[2], (D_MODEL,), f32)
    norm_in_w = 1.0 + 0.02 * jax.random.normal(ks[3], (DEPTH, D_MODEL), f32)
    w_in = jax.random.normal(ks[4], (DEPTH, D_MODEL, IN_W), f32) * D_MODEL ** -0.5
    attn_sink = 0.5 * jax.random.normal(ks[5], (DEPTH, ATT_HEADS), f32)
    conv_w = jax.random.normal(ks[6], (DEPTH, SSD_CONV, XBC_W), f32) * SSD_CONV ** -0.5
    conv_b = 0.01 * jax.random.normal(ks[7], (DEPTH, XBC_W), f32)
    u = jax.random.uniform(ks[8], (DEPTH, N_DIR, SSD_HEADS), f32)
    dt0 = jnp.exp(u * (math.log(0.1) - math.log(0.001)) + math.log(0.001))
    dt_bias = dt0 + jnp.log(-jnp.expm1(-dt0))
    a_log = jnp.log(jax.random.uniform(ks[9], (DEPTH, N_DIR, SSD_HEADS), f32,
                                       minval=1.0, maxval=16.0))
    d_skip = 1.0 + 0.1 * jax.random.normal(ks[10], (DEPTH, SSD_HEADS), f32)
    ssd_norm_w = 1.0 + 0.02 * jax.random.normal(ks[11], (DEPTH, SSD_W), f32)
    w_mem_kv = jax.random.normal(ks[12], (DEPTH, D_MODEL, 2 * MEM_W), f32) * D_MODEL ** -0.5
    w_out = jax.random.normal(ks[13], (DEPTH, MIX_W, D_MODEL), f32) * MIX_W ** -0.5
    norm_out_w = 1.0 + 0.02 * jax.random.normal(ks[14], (D_MODEL,), f32)
    return {"x": x, "mem": mem, "norm_mem_w": norm_mem_w, "norm_in_w": norm_in_w,
            "w_in": w_in, "attn_sink": attn_sink, "conv_w": conv_w, "conv_b": conv_b,
            "dt_bias": dt_bias, "a_log": a_log, "d_skip": d_skip, "ssd_norm_w": ssd_norm_w,
            "w_mem_kv": w_mem_kv, "w_out": w_out, "norm_out_w": norm_out_w}


def reference(x, mem, norm_mem_w, norm_in_w, w_in, attn_sink, conv_w, conv_b,
              dt_bias, a_log, d_skip, ssd_norm_w, w_mem_kv, w_out, norm_out_w):
    bsz, l, _ = x.shape
    pos = jnp.arange(l, dtype=jnp.float32)
    inv_freq = 1.0 / (ROPE_THETA ** (jnp.arange(0, ATT_HEAD_DIM, 2, dtype=jnp.float32) / ATT_HEAD_DIM))
    ang = pos[:, None] * inv_freq[None, :]
    cos, sin = jnp.cos(ang), jnp.sin(ang)
    memn = rms_norm(mem, norm_mem_w)
    splits = _split_points()

    h = x
    for layer in range(DEPTH):
        hn = rms_norm(h, norm_in_w[layer])
        proj = hn @ w_in[layer]
        q, k, v, g_attn, z, xbc, dt_raw, q_mem, g_mem = jnp.split(proj, splits, axis=-1)

        q = rotary(q.reshape(bsz, l, ATT_HEADS, ATT_HEAD_DIM), cos, sin)
        k = rotary(k.reshape(bsz, l, ATT_KV_HEADS, ATT_HEAD_DIM), cos, sin)
        v = v.reshape(bsz, l, ATT_KV_HEADS, ATT_HEAD_DIM)
        att = window_attention(q, k, v, attn_sink[layer]).reshape(bsz, l, ATT_W)
        att = att * jax.nn.silu(g_attn)

        xbc = jax.nn.silu(centred_depthwise_conv(xbc, conv_w[layer], conv_b[layer]))
        xs, bm, cm = jnp.split(xbc, [SSD_W, SSD_W + SSD_GROUPS * SSD_STATE], axis=-1)
        xs = xs.reshape(bsz, l, SSD_GROUPS, SSD_HPG, SSD_HEAD_DIM)
        bm = bm.reshape(bsz, l, SSD_GROUPS, SSD_STATE)
        cm = cm.reshape(bsz, l, SSD_GROUPS, SSD_STATE)
        dt = jax.nn.softplus(dt_raw.reshape(bsz, l, N_DIR, SSD_GROUPS, SSD_HPG)
                             + dt_bias[layer].reshape(N_DIR, SSD_GROUPS, SSD_HPG).astype(dt_raw.dtype))
        a = -jnp.exp(a_log[layer].astype(jnp.float32)).reshape(N_DIR, SSD_GROUPS, SSD_HPG)
        y_fwd = ssd_scan(xs, dt[:, :, 0], a[0], bm, cm)
        y_bwd = ssd_scan(xs[:, ::-1], dt[:, ::-1, 1], a[1], bm[:, ::-1], cm[:, ::-1])[:, ::-1]
        y = y_fwd + y_bwd + d_skip[layer].reshape(SSD_GROUPS, SSD_HPG, 1).astype(xs.dtype) * xs
        ssd = gated_group_rmsnorm(y.reshape(bsz, l, SSD_W).astype(x.dtype), z, ssd_norm_w[layer])

        mkv = memn @ w_mem_kv[layer]
        mk, mv = jnp.split(mkv, 2, axis=-1)
        mk = mk.reshape(bsz, MEM_LEN, MEM_HEADS, MEM_HEAD_DIM)
        mv = mv.reshape(bsz, MEM_LEN, MEM_HEADS, MEM_HEAD_DIM)
        qm = q_mem.reshape(bsz, l, MEM_HEADS, MEM_HEAD_DIM)
        sm = jnp.einsum('blhd,bmhd->bhlm', qm, mk).astype(jnp.float32) * MEM_HEAD_DIM ** -0.5
        pm = jax.nn.softmax(sm, axis=-1).astype(mv.dtype)
        xat = jnp.einsum('bhlm,bmhd->blhd', pm, mv).reshape(bsz, l, MEM_W)
        xat = xat * jax.nn.silu(g_mem)

        mixed = jnp.concatenate([att, ssd.astype(att.dtype), xat], axis=-1)
        h = h + mixed @ w_out[layer]

    return rms_norm(h, norm_out_w)
```

```python
import functools
import math

import jax
import jax.numpy as jnp
import numpy as np
from jax import lax
from jax.experimental import pallas as pl
from jax.experimental.pallas import tpu as pltpu

F32 = jnp.float32
BF16 = jnp.bfloat16

D_MODEL = 1024
ATT_HEADS = 16
ATT_KV_HEADS = 4
ATT_HEAD_DIM = 64
HALF = ATT_HEAD_DIM // 2
ATT_W = ATT_HEADS * ATT_HEAD_DIM
KV_W = ATT_KV_HEADS * ATT_HEAD_DIM
WINDOW = 128
BLOCK = 128
ROPE_THETA = 10000.0
SSD_HEADS = 8
SSD_HEAD_DIM = 64
SSD_W = SSD_HEADS * SSD_HEAD_DIM
SSD_GROUPS = 2
SSD_HPG = SSD_HEADS // SSD_GROUPS
SSD_STATE = 128
SSD_CONV = 5
CHUNK = 128
N_DIR = 2
XBC_W = SSD_W + 2 * SSD_GROUPS * SSD_STATE
MEM_LEN = 256
MEM_HEADS = 4
MEM_HEAD_DIM = 128
MEM_W = MEM_HEADS * MEM_HEAD_DIM
MIX_W = ATT_W + SSD_W + MEM_W
EPS = 1e-6

LANES = 128
GROUP_W = 256
HALO = 16
NEG = -1e30
VMEM_LIMIT = 56 * 1024 * 1024

PROJ_TM = 256
MIX_TQ = 256


def _silu(v):
    return v * (1.0 / (1.0 + jnp.exp(-v)))


def _softplus(v):
    return jnp.maximum(v, 0.0) + jnp.log(1.0 + jnp.exp(-jnp.abs(v)))


def _dot(a, b):
    return jnp.dot(a, b, preferred_element_type=F32)


def _dot_nt(a, b):
    return lax.dot_general(a, b, (((1,), (1,)), ((), ())), preferred_element_type=F32)


def _split3(v):
    hi = v.astype(BF16)
    r1 = v - hi.astype(F32)
    mid = r1.astype(BF16)
    lo = (r1 - mid.astype(F32)).astype(BF16)
    return hi, mid, lo


def _memkv_kernel(mem_ref, nw_ref, w_ref, mk_ref, mv_ref):
    m = mem_ref[0]
    ms = jnp.mean(m * m, axis=-1, keepdims=True)
    mn = (m * lax.rsqrt(ms + EPS) * nw_ref[...]).astype(BF16)
    mk_ref[0] = _dot(mn, w_ref[:, :MEM_W]).astype(BF16)
    mv_ref[0] = _dot(mn, w_ref[:, MEM_W:]).astype(BF16)


def _memkv(mem, norm_mem_w, w_mem_kv_bf):
    b = mem.shape[0]
    return pl.pallas_call(
        _memkv_kernel,
        grid=(b,),
        in_specs=[
            pl.BlockSpec((1, MEM_LEN, D_MODEL), lambda i: (i, 0, 0)),
            pl.BlockSpec((1, D_MODEL), lambda i: (0, 0)),
            pl.BlockSpec((D_MODEL, 2 * MEM_W), lambda i: (0, 0)),
        ],
        out_specs=[
            pl.BlockSpec((1, MEM_LEN, MEM_W), lambda i: (i, 0, 0)),
            pl.BlockSpec((1, MEM_LEN, MEM_W), lambda i: (i, 0, 0)),
        ],
        out_shape=[jax.ShapeDtypeStruct((b, MEM_LEN, MEM_W), BF16)] * 2,
        compiler_params=pltpu.CompilerParams(
            dimension_semantics=("arbitrary",), vmem_limit_bytes=VMEM_LIMIT),
        name="memkv",
    )(mem, norm_mem_w.reshape(1, D_MODEL), w_mem_kv_bf)


def _proj_kernel(x_ref, nw_ref, cos_ref, sin_ref,
                 wq_ref, wk_ref, wv_ref, wga_ref, wz_ref, wxbc_ref, wdt_ref, wqm_ref, wgm_ref,
                 q_out, k_out, v_out, ga_out, z_out, xbc_out, dt_out, qm_out, gm_out):
    x = x_ref[...]
    ms = jnp.mean(x * x, axis=-1, keepdims=True)
    hn = (x * lax.rsqrt(ms + EPS) * nw_ref[...]).astype(BF16)
    c = cos_ref[...]
    s = sin_ref[...]
    att_scale = ATT_HEAD_DIM ** -0.5
    mem_scale = MEM_HEAD_DIM ** -0.5

    for g in range(ATT_KV_HEADS):
        t = _dot(hn, wq_ref[:, GROUP_W * g:GROUP_W * (g + 1)])
        t1 = t[:, :LANES]
        t2 = t[:, LANES:]
        q_out[:, GROUP_W * g:GROUP_W * g + LANES] = ((t1 * c - t2 * s) * att_scale).astype(BF16)
        q_out[:, GROUP_W * g + LANES:GROUP_W * (g + 1)] = ((t1 * s + t2 * c) * att_scale).astype(BF16)

    t = _dot(hn, wk_ref[...])
    t1 = t[:, :LANES]
    t2 = t[:, LANES:]
    ko = (t1 * c - t2 * s, t1 * s + t2 * c)
    slot32 = lax.broadcasted_iota(jnp.int32, t1.shape, 1) // HALF
    for half in range(2):
        r = [ko[half]] + [pltpu.roll(ko[half], HALF * m, 1) for m in range(1, 4)]
        for g in range(ATT_KV_HEADS):
            rep = r[(3 - g) % 4]
            for j in (2, 1, 0):
                rep = jnp.where(slot32 == j, r[(j - g) % 4], rep)
            k_out[:, GROUP_W * g + LANES * half:GROUP_W * g + LANES * (half + 1)] = rep.astype(BF16)

    t = _dot(hn, wv_ref[...])
    low = lax.broadcasted_iota(jnp.int32, (t.shape[0], LANES), 1) < ATT_HEAD_DIM
    for pair in range(2):
        a = t[:, LANES * pair:LANES * (pair + 1)]
        ra = pltpu.roll(a, ATT_HEAD_DIM, 1)
        even = jnp.where(low, a, ra).astype(BF16)
        odd = jnp.where(low, ra, a).astype(BF16)
        for rep in range(2):
            v_out[:, GROUP_W * (2 * pair) + LANES * rep:GROUP_W * (2 * pair) + LANES * (rep + 1)] = even
            v_out[:, GROUP_W * (2 * pair + 1) + LANES * rep:GROUP_W * (2 * pair + 1) + LANES * (rep + 1)] = odd

    for j in range(ATT_W // GROUP_W):
        t = _dot(hn, wga_ref[:, GROUP_W * j:GROUP_W * (j + 1)])
        ga_out[:, GROUP_W * j:GROUP_W * (j + 1)] = _silu(t).astype(BF16)
    for j in range(SSD_W // GROUP_W):
        t = _dot(hn, wz_ref[:, GROUP_W * j:GROUP_W * (j + 1)])
        z_out[:, GROUP_W * j:GROUP_W * (j + 1)] = _silu(t).astype(BF16)
    for j in range(XBC_W // GROUP_W):
        t = _dot(hn, wxbc_ref[:, GROUP_W * j:GROUP_W * (j + 1)])
        xbc_out[:, GROUP_W * j:GROUP_W * (j + 1)] = t.astype(BF16)
    dt_out[...] = _dot(hn, wdt_ref[...])
    for j in range(MEM_W // GROUP_W):
        t = _dot(hn, wqm_ref[:, GROUP_W * j:GROUP_W * (j + 1)])
        qm_out[:, GROUP_W * j:GROUP_W * (j + 1)] = (t * mem_scale).astype(BF16)
        t = _dot(hn, wgm_ref[:, GROUP_W * j:GROUP_W * (j + 1)])
        gm_out[:, GROUP_W * j:GROUP_W * (j + 1)] = _silu(t).astype(BF16)


def _proj(x2, norm_w, cos128, sin128, weights, seq_len):
    t_rows = x2.shape[0]
    tm = PROJ_TM
    tiles_per_seq = seq_len // tm
    widths = (ATT_W, ATT_W, ATT_W, ATT_W, SSD_W, XBC_W, LANES, MEM_W, MEM_W)
    dtypes = (BF16, BF16, BF16, BF16, BF16, BF16, F32, BF16, BF16)
    row = lambda i: (i, 0)
    const = lambda i: (0, 0)
    in_specs = [
        pl.BlockSpec((tm, D_MODEL), row),
        pl.BlockSpec((1, D_MODEL), const),
        pl.BlockSpec((tm, LANES), lambda i: (i % tiles_per_seq, 0)),
        pl.BlockSpec((tm, LANES), lambda i: (i % tiles_per_seq, 0)),
    ] + [pl.BlockSpec(w.shape, const) for w in weights]
    return pl.pallas_call(
        _proj_kernel,
        grid=(t_rows // tm,),
        in_specs=in_specs,
        out_specs=[pl.BlockSpec((tm, w), row) for w in widths],
        out_shape=[jax.ShapeDtypeStruct((t_rows, w), d) for w, d in zip(widths, dtypes)],
        compiler_params=pltpu.CompilerParams(
            dimension_semantics=("arbitrary",), vmem_limit_bytes=VMEM_LIMIT),
        name="proj",
    )(x2, norm_w.reshape(1, D_MODEL), cos128, sin128, *weights)


def _slot_expand(mat, cols):
    slot = lax.broadcasted_iota(jnp.int32, (mat.shape[0], GROUP_W), 1) // SSD_HEAD_DIM
    out = jnp.broadcast_to(mat[:, cols[3]:cols[3] + 1], (mat.shape[0], GROUP_W))
    for j in (2, 1, 0):
        out = jnp.where(slot == j, jnp.broadcast_to(mat[:, cols[j]:cols[j] + 1], out.shape), out)
    return out


def _ssd_kernel(xl_ref, xm_ref, xr_ref, z_ref, dt_ref, cw_ref, cb_ref, dtb_ref, a_ref, dsk_ref, nw_ref,
                y_ref, ext_ref, xc_all, gb_all, hf_ref, gb_ref, *, n_chunks):
    t = pl.program_id(1)
    bwd = t < n_chunks
    c = jnp.where(bwd, n_chunks - 1 - t, t - n_chunks)

    dt = _softplus(dt_ref[...] + dtb_ref[...])
    adt = dt * a_ref[...]
    rr = lax.broadcasted_iota(jnp.int32, (CHUNK, CHUNK), 0)
    cc = lax.broadcasted_iota(jnp.int32, (CHUNK, CHUNK), 1)
    lower = rr >= cc
    upper = rr <= cc
    tri = jnp.where(lower, 1.0, 0.0).astype(BF16)
    adt_parts = _split3(adt)
    cs = sum(_dot(tri, p) for p in adt_parts)
    ex = cs - adt
    ones = jnp.ones((CHUNK, CHUNK), BF16)
    tot = sum(_dot(ones, p) for p in adt_parts)
    fwd_col = cc < SSD_HEADS
    row_scale = jnp.exp(jnp.where(fwd_col, cs, tot - ex))
    st_weight = jnp.exp(jnp.where(fwd_col, tot - cs, ex)) * dt
    decay = jnp.exp(tot)

    @pl.when(t == 0)
    def _():
        gb_ref[...] = jnp.zeros_like(gb_ref)

    @pl.when(t == n_chunks)
    def _():
        hf_ref[...] = jnp.zeros_like(hf_ref)

    slot = lax.broadcasted_iota(jnp.int32, (CHUNK, GROUP_W), 1) // SSD_HEAD_DIM

    def state_update(st_ref, xc, col0):
        for g in range(SSD_GROUPS):
            cols = [col0 + SSD_HPG * g + j for j in range(SSD_HPG)]
            xs = xc[:, GROUP_W * g:GROUP_W * (g + 1)]
            bm = xc[:, SSD_W + SSD_STATE * g:SSD_W + SSD_STATE * (g + 1)]
            xw = (xs * _slot_expand(st_weight, cols)).astype(BF16)
            upd = _dot(bm.T.astype(BF16), xw)
            st_ref[g] = st_ref[g] * _slot_expand(decay, cols) + upd

    @pl.when(bwd)
    def _():
        first = c == 0
        last = c == n_chunks - 1
        ext_ref[0:HALO, :] = xl_ref[0].astype(F32) * jnp.where(first, 0.0, 1.0)
        ext_ref[HALO:HALO + CHUNK, :] = xm_ref[0].astype(F32)
        ext_ref[HALO + CHUNK:, :] = xr_ref[0].astype(F32) * jnp.where(last, 0.0, 1.0)
        acc = jnp.broadcast_to(cb_ref[...], (CHUNK, XBC_W))
        pad = (SSD_CONV - 1) // 2
        for j in range(SSD_CONV):
            acc = acc + cw_ref[j:j + 1, :] * ext_ref[pl.ds(HALO - pad + j, CHUNK), :]
        xc = _silu(acc)
        xc_all[c] = xc.astype(BF16)
        gb_all[c] = gb_ref[...].astype(BF16)
        state_update(gb_ref, xc, SSD_HEADS)

    @pl.when(jnp.logical_not(bwd))
    def _():
        xc = xc_all[c].astype(F32)
        cs_t = cs.T
        ex_t = ex.T
        dt_t = dt.T
        for g in range(SSD_GROUPS):
            xs = xc[:, GROUP_W * g:GROUP_W * (g + 1)]
            bm = xc[:, SSD_W + SSD_STATE * g:SSD_W + SSD_STATE * (g + 1)].astype(BF16)
            cm = xc[:, SSD_W + SSD_GROUPS * SSD_STATE + SSD_STATE * g:
                    SSD_W + SSD_GROUPS * SSD_STATE + SSD_STATE * (g + 1)].astype(BF16)
            gmat = _dot_nt(cm, bm)
            xs_bf = xs.astype(BF16)
            m_parts = []
            x_parts = []
            for j in range(SSD_HPG):
                hf_col = SSD_HPG * g + j
                hb_col = SSD_HEADS + hf_col
                arg_f = jnp.minimum(cs[:, hf_col:hf_col + 1] - cs_t[hf_col:hf_col + 1, :], 0.0)
                arg_b = jnp.minimum(ex_t[hb_col:hb_col + 1, :] - ex[:, hb_col:hb_col + 1], 0.0)
                w_f = jnp.where(lower, jnp.exp(arg_f) * dt_t[hf_col:hf_col + 1, :], 0.0)
                w_b = jnp.where(upper, jnp.exp(arg_b) * dt_t[hb_col:hb_col + 1, :], 0.0)
                m_parts.append((gmat * (w_f + w_b)).astype(BF16))
                x_parts.append(jnp.where(slot == j, xs_bf, jnp.zeros_like(xs_bf)))
            m_cat = jnp.concatenate(m_parts, axis=1)
            x_stack = jnp.concatenate(x_parts, axis=0)
            y = _dot(m_cat, x_stack)
            cols_f = [SSD_HPG * g + j for j in range(SSD_HPG)]
            cols_b = [SSD_HEADS + v for v in cols_f]
            y = y + _dot(cm, hf_ref[g].astype(BF16)) * _slot_expand(row_scale, cols_f)
            y = y + _dot(cm, gb_all[c, g]) * _slot_expand(row_scale, cols_b)
            y = y + dsk_ref[:, GROUP_W * g:GROUP_W * (g + 1)] * xs
            gated = y * z_ref[0, :, GROUP_W * g:GROUP_W * (g + 1)].astype(F32)
            ms = jnp.mean(gated * gated, axis=-1, keepdims=True)
            y_ref[0, :, GROUP_W * g:GROUP_W * (g + 1)] = (
                gated * lax.rsqrt(ms + EPS) * nw_ref[:, GROUP_W * g:GROUP_W * (g + 1)]).astype(BF16)
        state_update(hf_ref, xc, 0)


def _ssd(xbc, zs, dtr, conv_w8, conv_b, dt_bias128, a128, dskip512, ssd_norm_w):
    b, seq_len, _ = xbc.shape
    n_chunks = seq_len // CHUNK
    halo_per_chunk = CHUNK // HALO
    n_halo = seq_len // HALO

    def chunk_of(t):
        return jnp.where(t < n_chunks, n_chunks - 1 - t, t - n_chunks)

    def fwd_chunk(t):
        return jnp.maximum(t - n_chunks, 0)

    def raw_chunk(t):
        return jnp.where(t < n_chunks, n_chunks - 1 - t, 0)

    const2 = lambda i, t: (0, 0)
    in_specs = [
        pl.BlockSpec((1, HALO, XBC_W), lambda i, t: (i, jnp.maximum(raw_chunk(t) * halo_per_chunk - 1, 0), 0)),
        pl.BlockSpec((1, CHUNK, XBC_W), lambda i, t: (i, raw_chunk(t), 0)),
        pl.BlockSpec((1, HALO, XBC_W),
                     lambda i, t: (i, jnp.minimum((raw_chunk(t) + 1) * halo_per_chunk, n_halo - 1), 0)),
        pl.BlockSpec((1, CHUNK, SSD_W), lambda i, t: (i, fwd_chunk(t), 0)),
        pl.BlockSpec((CHUNK, LANES), lambda i, t: (i * n_chunks + chunk_of(t), 0)),
        pl.BlockSpec((8, XBC_W), const2),
        pl.BlockSpec((1, XBC_W), const2),
        pl.BlockSpec((1, LANES), const2),
        pl.BlockSpec((1, LANES), const2),
        pl.BlockSpec((1, SSD_W), const2),
        pl.BlockSpec((1, SSD_W), const2),
    ]
    return pl.pallas_call(
        functools.partial(_ssd_kernel, n_chunks=n_chunks),
        grid=(b, 2 * n_chunks),
        in_specs=in_specs,
        out_specs=pl.BlockSpec((1, CHUNK, SSD_W), lambda i, t: (i, fwd_chunk(t), 0)),
        out_shape=jax.ShapeDtypeStruct((b, seq_len, SSD_W), BF16),
        scratch_shapes=[
            pltpu.VMEM((CHUNK + 2 * HALO, XBC_W), F32),
            pltpu.VMEM((n_chunks, CHUNK, XBC_W), BF16),
            pltpu.VMEM((n_chunks, SSD_GROUPS, SSD_STATE, GROUP_W), BF16),
            pltpu.VMEM((SSD_GROUPS, SSD_STATE, GROUP_W), F32),
            pltpu.VMEM((SSD_GROUPS, SSD_STATE, GROUP_W), F32),
        ],
        compiler_params=pltpu.CompilerParams(
            dimension_semantics=("arbitrary", "arbitrary"), vmem_limit_bytes=VMEM_LIMIT),
        name="ssd",
    )(xbc, xbc, xbc, zs, dtr, conv_w8, conv_b, dt_bias128, a128, dskip512, ssd_norm_w)


def _mix_kernel(x_ref, q_ref, kl_ref, km_ref, kr_ref, vl_ref, vm_ref, vr_ref, ga_ref,
                qm_ref, gm_ref, mk_ref, mv_ref, ssd_ref, sink_ref, wo_ref, now_ref,
                o_ref, kst_ref, vst_ref, p_ref, att_ref, xat_ref, *, n_tiles):
    i = pl.program_id(1)
    tq = q_ref.shape[1]
    nblk = tq // BLOCK
    span = BLOCK + 2 * WINDOW

    rr = lax.broadcasted_iota(jnp.int32, (BLOCK, WINDOW), 0)
    cc = lax.broadcasted_iota(jnp.int32, (BLOCK, WINDOW), 1)
    left_bias0 = jnp.where(cc >= rr, 0.0, NEG)
    right_bias0 = jnp.where(cc <= rr, 0.0, NEG)
    slot = lax.broadcasted_iota(jnp.int32, (span, GROUP_W), 1) % LANES // HALF
    vslot = lax.broadcasted_iota(jnp.int32, (span, GROUP_W), 1) // ATT_HEAD_DIM
    oslot = lax.broadcasted_iota(jnp.int32, (BLOCK, GROUP_W), 1) // ATT_HEAD_DIM
    kmask = [jnp.where(slot == j, 1.0, 0.0).astype(BF16) for j in range(4)]
    vmask = [jnp.where(vslot == j, 1.0, 0.0).astype(BF16) for j in range(4)]

    for blk in range(nblk):
        r0 = blk * BLOCK
        def piece(l_ref, m_ref, r_ref, which, g):
            lo = GROUP_W * g
            if which == 0:
                if blk == 0:
                    return l_ref[0, :, lo:lo + GROUP_W]
                return m_ref[0, r0 - BLOCK:r0, lo:lo + GROUP_W]
            if which == 1:
                return m_ref[0, r0:r0 + BLOCK, lo:lo + GROUP_W]
            if blk == nblk - 1:
                return r_ref[0, :, lo:lo + GROUP_W]
            return m_ref[0, r0 + BLOCK:r0 + 2 * BLOCK, lo:lo + GROUP_W]

        left_dead = jnp.logical_and(i == 0, blk == 0)
        right_dead = jnp.logical_and(i == n_tiles - 1, blk == nblk - 1)
        left_bias = left_bias0 + jnp.where(left_dead, NEG, 0.0)
        right_bias = right_bias0 + jnp.where(right_dead, NEG, 0.0)

        for g in range(ATT_KV_HEADS):
            kwin = jnp.concatenate([piece(kl_ref, km_ref, kr_ref, w, g) for w in range(3)], axis=0)
            vwin = jnp.concatenate([piece(vl_ref, vm_ref, vr_ref, w, g) for w in range(3)], axis=0)
            for j in range(4):
                kst_ref[span * j:span * (j + 1), :] = kwin * kmask[j]
                vst_ref[span * j:span * (j + 1), :] = vwin * vmask[j]
            qg = q_ref[0, r0:r0 + BLOCK, GROUP_W * g:GROUP_W * (g + 1)]
            s_cat = _dot_nt(qg, kst_ref[...])
            inv = jnp.zeros((BLOCK, GROUP_W), F32)
            for j in range(4):
                h = 4 * g + j
                s_l = s_cat[:, span * j:span * j + WINDOW] + left_bias
                s_m = s_cat[:, span * j + WINDOW:span * j + WINDOW + BLOCK]
                s_r = s_cat[:, span * j + WINDOW + BLOCK:span * (j + 1)] + right_bias
                sink = sink_ref[0, h]
                m = jnp.maximum(jnp.max(jnp.maximum(jnp.maximum(s_l, s_m), s_r), axis=-1, keepdims=True), sink)
                p_l = jnp.exp(s_l - m)
                p_m = jnp.exp(s_m - m)
                p_r = jnp.exp(s_r - m)
                den = jnp.sum(p_l + p_m + p_r, axis=-1, keepdims=True) + jnp.exp(sink - m)
                p_ref[:, span * j:span * j + WINDOW] = p_l.astype(BF16)
                p_ref[:, span * j + WINDOW:span * j + WINDOW + BLOCK] = p_m.astype(BF16)
                p_ref[:, span * j + WINDOW + BLOCK:span * (j + 1)] = p_r.astype(BF16)
                inv = jnp.where(oslot == j, jnp.broadcast_to(1.0 / den, inv.shape), inv)
            o = _dot(p_ref[...], vst_ref[...]) * inv
            gate = ga_ref[0, r0:r0 + BLOCK, GROUP_W * g:GROUP_W * (g + 1)].astype(F32)
            att_ref[r0:r0 + BLOCK, GROUP_W * g:GROUP_W * (g + 1)] = (o * gate).astype(BF16)

    for h in range(MEM_HEADS):
        lo = MEM_HEAD_DIM * h
        s = _dot_nt(qm_ref[0, :, lo:lo + MEM_HEAD_DIM], mk_ref[0, :, lo:lo + MEM_HEAD_DIM])
        m = jnp.max(s, axis=-1, keepdims=True)
        p = jnp.exp(s - m)
        den = jnp.sum(p, axis=-1, keepdims=True)
        o = _dot(p.astype(BF16), mv_ref[0, :, lo:lo + MEM_HEAD_DIM]) * (1.0 / den)
        xat_ref[:, lo:lo + MEM_HEAD_DIM] = (o * gm_ref[0, :, lo:lo + MEM_HEAD_DIM].astype(F32)).astype(BF16)

    delta = _dot(att_ref[...], wo_ref[0:ATT_W, :])
    delta = delta + _dot(ssd_ref[0], wo_ref[ATT_W:ATT_W + SSD_W, :])
    delta = delta + _dot(xat_ref[...], wo_ref[ATT_W + SSD_W:, :])
    hres = x_ref[0] + delta
    ms = jnp.mean(hres * hres, axis=-1, keepdims=True)
    o_ref[0] = hres * lax.rsqrt(ms + EPS) * now_ref[...]


def _mix(x, q, k, v, ga, qm, gm, mk, mv, ssd, sink, w_out_bf, norm_out_w):
    b, seq_len, _ = x.shape
    tq = MIX_TQ
    n_tiles = seq_len // tq
    bpt = tq // BLOCK
    n_blocks = seq_len // BLOCK
    span = BLOCK + 2 * WINDOW
    main = lambda w: pl.BlockSpec((1, tq, w), lambda bi, i: (bi, i, 0))
    left = lambda w: pl.BlockSpec((1, BLOCK, w), lambda bi, i: (bi, jnp.maximum(i * bpt - 1, 0), 0))
    right = lambda w: pl.BlockSpec((1, BLOCK, w), lambda bi, i: (bi, jnp.minimum((i + 1) * bpt, n_blocks - 1), 0))
    per_b = lambda r, w: pl.BlockSpec((1, r, w), lambda bi, i: (bi, 0, 0))
    const2 = lambda bi, i: (0, 0)
    in_specs = [
        main(D_MODEL), main(ATT_W),
        left(ATT_W), main(ATT_W), right(ATT_W),
        left(ATT_W), main(ATT_W), right(ATT_W),
        main(ATT_W), main(MEM_W), main(MEM_W),
        per_b(MEM_LEN, MEM_W), per_b(MEM_LEN, MEM_W),
        main(SSD_W),
        pl.BlockSpec(memory_space=pltpu.SMEM),
        pl.BlockSpec((MIX_W, D_MODEL), const2),
        pl.BlockSpec((1, D_MODEL), const2),
    ]
    return pl.pallas_call(
        functools.partial(_mix_kernel, n_tiles=n_tiles),
        grid=(b, n_tiles),
        in_specs=in_specs,
        out_specs=pl.BlockSpec((1, tq, D_MODEL), lambda bi, i: (bi, i, 0)),
        out_shape=jax.ShapeDtypeStruct((b, seq_len, D_MODEL), F32),
        scratch_shapes=[
            pltpu.VMEM((4 * span, GROUP_W), BF16),
            pltpu.VMEM((4 * span, GROUP_W), BF16),
            pltpu.VMEM((BLOCK, 4 * span), BF16),
            pltpu.VMEM((tq, ATT_W), BF16),
            pltpu.VMEM((tq, MEM_W), BF16),
        ],
        compiler_params=pltpu.CompilerParams(
            dimension_semantics=("arbitrary", "arbitrary"), vmem_limit_bytes=VMEM_LIMIT),
        name="mix",
    )(x, q, k, k, k, v, v, v, ga, qm, gm, mk, mv, ssd, sink, w_out_bf, norm_out_w.reshape(1, D_MODEL))


def _q_perm():
    idx = np.zeros((ATT_W,), np.int32)
    for g in range(ATT_KV_HEADS):
        for half in range(2):
            for j in range(4):
                for d in range(HALF):
                    idx[GROUP_W * g + LANES * half + HALF * j + d] = (4 * g + j) * ATT_HEAD_DIM + HALF * half + d
    return idx


def _k_perm():
    idx = np.zeros((KV_W,), np.int32)
    for half in range(2):
        for g in range(ATT_KV_HEADS):
            for d in range(HALF):
                idx[LANES * half + HALF * g + d] = g * ATT_HEAD_DIM + HALF * half + d
    return idx


def kernel(x, mem, norm_mem_w, norm_in_w, w_in, attn_sink, conv_w, conv_b, dt_bias, a_log, d_skip,
           ssd_norm_w, w_mem_kv, w_out, norm_out_w):
    b, seq_len, _ = x.shape
    depth = w_in.shape[0]

    pos = jnp.arange(seq_len, dtype=F32)
    inv_freq = 1.0 / (ROPE_THETA ** (jnp.arange(0, ATT_HEAD_DIM, 2, dtype=F32) / ATT_HEAD_DIM))
    ang = pos[:, None] * inv_freq[None, :]
    cos128 = jnp.tile(jnp.cos(ang), (1, 4))
    sin128 = jnp.tile(jnp.sin(ang), (1, 4))

    assert depth == 1, "the final norm is fused into the single layer's output stage"
    layer = 0
    w = w_in[layer]
    o = 0
    parts = []
    for width in (ATT_W, KV_W, KV_W, ATT_W, SSD_W, XBC_W, N_DIR * SSD_HEADS, MEM_W, MEM_W):
        parts.append(w[:, o:o + width])
        o += width
    wq, wk, wv, wga, wz, wxbc, wdt, wqm, wgm = parts
    wq = wq[:, _q_perm()]
    wk = wk[:, _k_perm()]
    pad16 = LANES - N_DIR * SSD_HEADS
    wdt = jnp.pad(wdt, ((0, 0), (0, pad16)))
    weights = [m.astype(BF16) for m in (wq, wk, wv, wga, wz, wxbc, wdt, wqm, wgm)]

    mk, mv = _memkv(mem, norm_mem_w, w_mem_kv[layer].astype(BF16))
    q, k, v, ga, zs, xbc, dtr, qm, gm = _proj(
        x.reshape(b * seq_len, D_MODEL), norm_in_w[layer], cos128, sin128, weights, seq_len)

    dt_bias128 = jnp.pad(dt_bias[layer].reshape(1, -1).astype(F32), ((0, 0), (0, pad16)))
    a128 = jnp.pad(-jnp.exp(a_log[layer].astype(F32)).reshape(1, -1), ((0, 0), (0, pad16)))
    dskip512 = jnp.repeat(d_skip[layer].astype(F32), SSD_HEAD_DIM).reshape(1, SSD_W)
    conv_w8 = jnp.pad(conv_w[layer].astype(F32), ((0, 8 - SSD_CONV), (0, 0)))
    ssd = _ssd(xbc.reshape(b, seq_len, XBC_W), zs.reshape(b, seq_len, SSD_W), dtr,
               conv_w8, conv_b[layer].reshape(1, XBC_W).astype(F32), dt_bias128, a128, dskip512,
               ssd_norm_w[layer].reshape(1, SSD_W).astype(F32))

    r3 = lambda arr: arr.reshape(b, seq_len, arr.shape[-1])
    return _mix(x, r3(q), r3(k), r3(v), r3(ga), r3(qm), r3(gm), mk, mv, ssd,
                attn_sink[layer].reshape(1, ATT_HEADS).astype(F32), w_out[layer].astype(BF16), norm_out_w)
```

```python
import functools
import math

import jax
import jax.numpy as jnp
import numpy as np
from jax import lax
from jax.experimental import pallas as pl
from jax.experimental.pallas import tpu as pltpu

F32 = jnp.float32
BF16 = jnp.bfloat16

D_MODEL = 1024
ATT_HEADS = 16
ATT_KV_HEADS = 4
ATT_HEAD_DIM = 64
HALF = ATT_HEAD_DIM // 2
ATT_W = ATT_HEADS * ATT_HEAD_DIM
KV_W = ATT_KV_HEADS * ATT_HEAD_DIM
WINDOW = 128
BLOCK = 128
ROPE_THETA = 10000.0
SSD_HEADS = 8
SSD_HEAD_DIM = 64
SSD_W = SSD_HEADS * SSD_HEAD_DIM
SSD_GROUPS = 2
SSD_HPG = SSD_HEADS // SSD_GROUPS
SSD_STATE = 128
SSD_CONV = 5
CHUNK = 128
N_DIR = 2
XBC_W = SSD_W + 2 * SSD_GROUPS * SSD_STATE
MEM_LEN = 256
MEM_HEADS = 4
MEM_HEAD_DIM = 128
MEM_W = MEM_HEADS * MEM_HEAD_DIM
MIX_W = ATT_W + SSD_W + MEM_W
EPS = 1e-6

LANES = 128
GROUP_W = 256
HALO = 16
NEG = -1e30
VMEM_LIMIT = 56 * 1024 * 1024
LOG2E = math.log2(math.e)

PROJ_TM = 256
MIX_TQ = 256
SSD_CPS = 2


def _silu(v):
    return v * (1.0 / (1.0 + jnp.exp(-v)))


def _softplus(v):
    return jnp.maximum(v, 0.0) + jnp.log(1.0 + jnp.exp(-jnp.abs(v)))


def _dot(a, b):
    return jnp.dot(a, b, preferred_element_type=F32)


def _dot_nt(a, b):
    return lax.dot_general(a, b, (((1,), (1,)), ((), ())), preferred_element_type=F32)


def _split3(v):
    hi = v.astype(BF16)
    r1 = v - hi.astype(F32)
    mid = r1.astype(BF16)
    lo = (r1 - mid.astype(F32)).astype(BF16)
    return hi, mid, lo


def _memkv_kernel(mem_ref, nw_ref, w_ref, mk_ref, mv_ref):
    m = mem_ref[0]
    ms = jnp.mean(m * m, axis=-1, keepdims=True)
    mn = (m * lax.rsqrt(ms + EPS) * nw_ref[...]).astype(BF16)
    mk_ref[0] = _dot(mn, w_ref[:, :MEM_W]).astype(BF16)
    mv_ref[0] = _dot(mn, w_ref[:, MEM_W:]).astype(BF16)


def _memkv(mem, norm_mem_w, w_mem_kv_bf):
    b = mem.shape[0]
    return pl.pallas_call(
        _memkv_kernel,
        grid=(b,),
        in_specs=[
            pl.BlockSpec((1, MEM_LEN, D_MODEL), lambda i: (i, 0, 0)),
            pl.BlockSpec((1, D_MODEL), lambda i: (0, 0)),
            pl.BlockSpec((D_MODEL, 2 * MEM_W), lambda i: (0, 0)),
        ],
        out_specs=[
            pl.BlockSpec((1, MEM_LEN, MEM_W), lambda i: (i, 0, 0)),
            pl.BlockSpec((1, MEM_LEN, MEM_W), lambda i: (i, 0, 0)),
        ],
        out_shape=[jax.ShapeDtypeStruct((b, MEM_LEN, MEM_W), BF16)] * 2,
        compiler_params=pltpu.CompilerParams(
            dimension_semantics=("arbitrary",), vmem_limit_bytes=VMEM_LIMIT),
        name="memkv",
    )(mem, norm_mem_w.reshape(1, D_MODEL), w_mem_kv_bf)


def _proj_kernel(x_ref, nw_ref, cos_ref, sin_ref,
                 wq_ref, wk_ref, wv_ref, wga_ref, wz_ref, wxbc_ref, wdt_ref, wqm_ref, wgm_ref,
                 q_out, k_out, v_out, ga_out, z_out, xbc_out, dt_out, qm_out, gm_out):
    x = x_ref[...]
    ms = jnp.mean(x * x, axis=-1, keepdims=True)
    hn = (x * lax.rsqrt(ms + EPS) * nw_ref[...]).astype(BF16)
    c = cos_ref[...]
    s = sin_ref[...]
    att_scale = ATT_HEAD_DIM ** -0.5
    mem_scale = MEM_HEAD_DIM ** -0.5

    for g in range(ATT_KV_HEADS):
        t = _dot(hn, wq_ref[:, GROUP_W * g:GROUP_W * (g + 1)])
        t1 = t[:, :LANES]
        t2 = t[:, LANES:]
        q_out[:, GROUP_W * g:GROUP_W * g + LANES] = ((t1 * c - t2 * s) * att_scale).astype(BF16)
        q_out[:, GROUP_W * g + LANES:GROUP_W * (g + 1)] = ((t1 * s + t2 * c) * att_scale).astype(BF16)

    t = _dot(hn, wk_ref[...])
    t1 = t[:, :LANES]
    t2 = t[:, LANES:]
    ko = (t1 * c - t2 * s, t1 * s + t2 * c)
    slot32 = lax.broadcasted_iota(jnp.int32, t1.shape, 1) // HALF
    for half in range(2):
        r = [ko[half]] + [pltpu.roll(ko[half], HALF * m, 1) for m in range(1, 4)]
        for g in range(ATT_KV_HEADS):
            rep = r[(3 - g) % 4]
            for j in (2, 1, 0):
                rep = jnp.where(slot32 == j, r[(j - g) % 4], rep)
            k_out[:, GROUP_W * g + LANES * half:GROUP_W * g + LANES * (half + 1)] = rep.astype(BF16)

    t = _dot(hn, wv_ref[...])
    low = lax.broadcasted_iota(jnp.int32, (t.shape[0], LANES), 1) < ATT_HEAD_DIM
    for pair in range(2):
        a = t[:, LANES * pair:LANES * (pair + 1)]
        ra = pltpu.roll(a, ATT_HEAD_DIM, 1)
        even = jnp.where(low, a, ra).astype(BF16)
        odd = jnp.where(low, ra, a).astype(BF16)
        for rep in range(2):
            v_out[:, GROUP_W * (2 * pair) + LANES * rep:GROUP_W * (2 * pair) + LANES * (rep + 1)] = even
            v_out[:, GROUP_W * (2 * pair + 1) + LANES * rep:GROUP_W * (2 * pair + 1) + LANES * (rep + 1)] = odd

    for j in range(ATT_W // GROUP_W):
        t = _dot(hn, wga_ref[:, GROUP_W * j:GROUP_W * (j + 1)])
        ga_out[:, GROUP_W * j:GROUP_W * (j + 1)] = _silu(t).astype(BF16)
    for j in range(SSD_W // GROUP_W):
        t = _dot(hn, wz_ref[:, GROUP_W * j:GROUP_W * (j + 1)])
        z_out[:, GROUP_W * j:GROUP_W * (j + 1)] = _silu(t).astype(BF16)
    for j in range(XBC_W // GROUP_W):
        t = _dot(hn, wxbc_ref[:, GROUP_W * j:GROUP_W * (j + 1)])
        xbc_out[:, GROUP_W * j:GROUP_W * (j + 1)] = t.astype(BF16)
    dt_out[...] = _dot(hn, wdt_ref[...])
    for j in range(MEM_W // GROUP_W):
        t = _dot(hn, wqm_ref[:, GROUP_W * j:GROUP_W * (j + 1)])
        qm_out[:, GROUP_W * j:GROUP_W * (j + 1)] = (t * mem_scale).astype(BF16)
        t = _dot(hn, wgm_ref[:, GROUP_W * j:GROUP_W * (j + 1)])
        gm_out[:, GROUP_W * j:GROUP_W * (j + 1)] = _silu(t).astype(BF16)


def _proj(x2, norm_w, cos128, sin128, weights, seq_len):
    t_rows = x2.shape[0]
    tm = PROJ_TM
    tiles_per_seq = seq_len // tm
    widths = (ATT_W, ATT_W, ATT_W, ATT_W, SSD_W, XBC_W, LANES, MEM_W, MEM_W)
    dtypes = (BF16, BF16, BF16, BF16, BF16, BF16, F32, BF16, BF16)
    row = lambda i: (i, 0)
    const = lambda i: (0, 0)
    in_specs = [
        pl.BlockSpec((tm, D_MODEL), row),
        pl.BlockSpec((1, D_MODEL), const),
        pl.BlockSpec((tm, LANES), lambda i: (i % tiles_per_seq, 0)),
        pl.BlockSpec((tm, LANES), lambda i: (i % tiles_per_seq, 0)),
    ] + [pl.BlockSpec(w.shape, const) for w in weights]
    return pl.pallas_call(
        _proj_kernel,
        grid=(t_rows // tm,),
        in_specs=in_specs,
        out_specs=[pl.BlockSpec((tm, w), row) for w in widths],
        out_shape=[jax.ShapeDtypeStruct((t_rows, w), d) for w, d in zip(widths, dtypes)],
        compiler_params=pltpu.CompilerParams(
            dimension_semantics=("arbitrary",), vmem_limit_bytes=VMEM_LIMIT),
        name="proj",
    )(x2, norm_w.reshape(1, D_MODEL), cos128, sin128, *weights)


def _ssd_kernel(xl_ref, xm_ref, xr_ref, z_ref, dt_ref, cw_ref, cb_ref, dtb_ref, a_ref, dsk_ref, nw_ref,
                shift_ref, ef_ref, eb_ref,
                y_ref,
                ext_ref, xc_all, gb_all, z_all, rt_all, rs_all, sw_all, tl_all, hf_ref, gb_ref, *, n_steps):
    t = pl.program_id(1)
    bwd = t < n_steps
    cps = xm_ref.shape[1] // CHUNK

    def state_update(st_ref, xc, sw_bf, tl8, e_ref):
        e = e_ref[...]
        wexp = _dot(sw_bf, e)
        dec = jnp.exp2(sum(_dot(p, e) for p in _split3(tl8)))
        for g in range(SSD_GROUPS):
            lo = GROUP_W * g
            xs = xc[:, lo:lo + GROUP_W]
            bm = xc[:, SSD_W + SSD_STATE * g:SSD_W + SSD_STATE * (g + 1)]
            xw = (xs * wexp[:, lo:lo + GROUP_W]).astype(BF16)
            upd = _dot(bm.T.astype(BF16), xw)
            st_ref[g] = st_ref[g] * dec[0:1, lo:lo + GROUP_W] + upd

    def sweep_left(c, k, step):
        rr = lax.broadcasted_iota(jnp.int32, (CHUNK, CHUNK), 0)
        cc = lax.broadcasted_iota(jnp.int32, (CHUNK, CHUNK), 1)
        fwd_col = cc < SSD_HEADS
        dt = _softplus(dt_ref[CHUNK * k:CHUNK * (k + 1), :] + dtb_ref[...])
        adt = dt * (a_ref[...] * LOG2E)
        tri = jnp.where(rr >= cc, 1.0, 0.0).astype(BF16)
        ones = jnp.ones((CHUNK, CHUNK), BF16)
        parts = _split3(adt)
        cs = sum(_dot(tri, p) for p in parts)
        tl = sum(_dot(ones, p) for p in parts)
        zl = jnp.where(fwd_col, cs, cs - adt)
        ldt = jnp.log2(dt)
        rt = jnp.where(fwd_col, zl - ldt, zl + ldt).T
        rs = jnp.exp2(jnp.where(fwd_col, zl, tl - zl)).astype(BF16)
        sw = (jnp.exp2(jnp.where(fwd_col, tl - zl, zl)) * dt).astype(BF16)
        z_all[c] = zl
        rt_all[c] = rt[0:N_DIR * SSD_HEADS, :]
        rs_all[c] = rs
        sw_all[c] = sw
        tl_all[c] = tl[0:8, :]

        ext = ext_ref[CHUNK * k:CHUNK * (k + 2), :]
        pad = (SSD_CONV - 1) // 2
        acc = cb_ref[...] + cw_ref[pad:pad + 1, :] * xm_ref[0, CHUNK * k:CHUNK * (k + 1), :].astype(F32)
        tap = 0
        for j in range(SSD_CONV):
            if j == pad:
                continue
            acc = acc + cw_ref[j:j + 1, :] * _dot(shift_ref[tap], ext)
            tap += 1
        xc = _silu(acc)
        xc_all[c] = xc.astype(BF16)
        gb_all[c] = gb_ref[...].astype(BF16)
        state_update(gb_ref, xc, sw, tl[0:8, :], eb_ref)

    @pl.when(bwd)
    def _():
        step = n_steps - 1 - t

        @pl.when(t == 0)
        def _():
            gb_ref[...] = jnp.zeros_like(gb_ref)
            tail = CHUNK * cps + 2 * HALO
            ext_ref[tail:, :] = jnp.zeros((ext_ref.shape[0] - tail, XBC_W), BF16)

        lflag = jnp.where(step == 0, 0.0, 1.0)
        rflag = jnp.where(step == n_steps - 1, 0.0, 1.0)
        ext_ref[0:HALO, :] = (xl_ref[0].astype(F32) * lflag).astype(BF16)
        ext_ref[HALO:HALO + CHUNK * cps, :] = xm_ref[0]
        ext_ref[HALO + CHUNK * cps:2 * HALO + CHUNK * cps, :] = (xr_ref[0].astype(F32) * rflag).astype(BF16)
        for k in reversed(range(cps)):
            sweep_left(step * cps + k, k, step)

    def sweep_right(c, k):
        rr = lax.broadcasted_iota(jnp.int32, (CHUNK, CHUNK), 0)
        cc = lax.broadcasted_iota(jnp.int32, (CHUNK, CHUNK), 1)
        lower = rr >= cc
        upper = rr <= cc
        slot = lax.broadcasted_iota(jnp.int32, (CHUNK, GROUP_W), 1) // SSD_HEAD_DIM
        slotmask = [jnp.where(slot == j, 1.0, 0.0).astype(BF16) for j in range(SSD_HPG)]

        xc = xc_all[c].astype(F32)
        zl = z_all[c]
        rt = rt_all[c]
        rs = rs_all[c]
        scale_f = _dot(rs, ef_ref[...])
        scale_b = _dot(rs, eb_ref[...])
        for g in range(SSD_GROUPS):
            lo = GROUP_W * g
            xs = xc[:, lo:lo + GROUP_W]
            bm = xc[:, SSD_W + SSD_STATE * g:SSD_W + SSD_STATE * (g + 1)].astype(BF16)
            cm = xc[:, SSD_W + SSD_GROUPS * SSD_STATE + SSD_STATE * g:
                    SSD_W + SSD_GROUPS * SSD_STATE + SSD_STATE * (g + 1)].astype(BF16)
            gmat = _dot_nt(cm, bm)
            xs_bf = xs.astype(BF16)
            m_parts = []
            x_parts = []
            for j in range(SSD_HPG):
                hf = SSD_HPG * g + j
                hb = SSD_HEADS + hf
                w_f = jnp.where(lower, jnp.exp2(zl[:, hf:hf + 1] - rt[hf:hf + 1, :]), 0.0)
                w_b = jnp.where(upper, jnp.exp2(rt[hb:hb + 1, :] - zl[:, hb:hb + 1]), 0.0)
                m_parts.append((gmat * (w_f + w_b)).astype(BF16))
                x_parts.append(xs_bf * slotmask[j])
            m_cat = jnp.concatenate(m_parts, axis=1)
            x_stack = jnp.concatenate(x_parts, axis=0)
            y = _dot(m_cat, x_stack)
            y = y + _dot(cm, hf_ref[g].astype(BF16)) * scale_f[:, lo:lo + GROUP_W]
            y = y + _dot(cm, gb_all[c, g]) * scale_b[:, lo:lo + GROUP_W]
            y = y + dsk_ref[:, lo:lo + GROUP_W] * xs
            gated = y * z_ref[0, CHUNK * k:CHUNK * (k + 1), lo:lo + GROUP_W].astype(F32)
            ms = jnp.mean(gated * gated, axis=-1, keepdims=True)
            y_ref[0, CHUNK * k:CHUNK * (k + 1), lo:lo + GROUP_W] = (
                gated * lax.rsqrt(ms + EPS) * nw_ref[:, lo:lo + GROUP_W]).astype(BF16)
        state_update(hf_ref, xc, sw_all[c], tl_all[c], ef_ref)

    @pl.when(jnp.logical_not(bwd))
    def _():
        step = t - n_steps

        @pl.when(t == n_steps)
        def _():
            hf_ref[...] = jnp.zeros_like(hf_ref)

        for k in range(cps):
            sweep_right(step * cps + k, k)


def _ssd(xbc, zs, dtr, conv_w8, conv_b, dt_bias128, a128, dskip512, ssd_norm_w):
    b, seq_len, _ = xbc.shape
    n_chunks = seq_len // CHUNK
    cps = SSD_CPS
    rows_per_step = CHUNK * cps
    n_steps = n_chunks // cps
    halo_per_step = rows_per_step // HALO
    n_halo = seq_len // HALO
    win_rows = 2 * CHUNK
    ext_rows = CHUNK * (cps + 1)
    pad = (SSD_CONV - 1) // 2

    rows = np.arange(CHUNK)[:, None]
    cols = np.arange(win_rows)[None, :]
    shift = np.stack([(cols == rows + HALO - pad + j) for j in range(SSD_CONV) if j != pad]).astype(np.float32)
    hcol = np.arange(LANES)[:, None]
    hslot = (np.arange(SSD_W) // SSD_HEAD_DIM)[None, :]
    e_f = (hcol == hslot).astype(np.float32)
    e_b = (hcol == hslot + SSD_HEADS).astype(np.float32)

    def fwd_step(t):
        return jnp.maximum(t - n_steps, 0)

    def raw_step(t):
        return jnp.where(t < n_steps, n_steps - 1 - t, 0)

    const2 = lambda i, t: (0, 0)
    const3 = lambda i, t: (0, 0, 0)
    in_specs = [
        pl.BlockSpec((1, HALO, XBC_W), lambda i, t: (i, jnp.maximum(raw_step(t) * halo_per_step - 1, 0), 0)),
        pl.BlockSpec((1, rows_per_step, XBC_W), lambda i, t: (i, raw_step(t), 0)),
        pl.BlockSpec((1, HALO, XBC_W),
                     lambda i, t: (i, jnp.minimum((raw_step(t) + 1) * halo_per_step, n_halo - 1), 0)),
        pl.BlockSpec((1, rows_per_step, SSD_W), lambda i, t: (i, fwd_step(t), 0)),
        pl.BlockSpec((rows_per_step, LANES), lambda i, t: (i * n_steps + raw_step(t), 0)),
        pl.BlockSpec((8, XBC_W), const2),
        pl.BlockSpec((1, XBC_W), const2),
        pl.BlockSpec((1, LANES), const2),
        pl.BlockSpec((1, LANES), const2),
        pl.BlockSpec((1, SSD_W), const2),
        pl.BlockSpec((1, SSD_W), const2),
        pl.BlockSpec((SSD_CONV - 1, CHUNK, win_rows), const3),
        pl.BlockSpec((LANES, SSD_W), const2),
        pl.BlockSpec((LANES, SSD_W), const2),
    ]
    return pl.pallas_call(
        functools.partial(_ssd_kernel, n_steps=n_steps),
        grid=(b, 2 * n_steps),
        in_specs=in_specs,
        out_specs=pl.BlockSpec((1, rows_per_step, SSD_W), lambda i, t: (i, fwd_step(t), 0)),
        out_shape=jax.ShapeDtypeStruct((b, seq_len, SSD_W), BF16),
        scratch_shapes=[
            pltpu.VMEM((ext_rows, XBC_W), BF16),
            pltpu.VMEM((n_chunks, CHUNK, XBC_W), BF16),
            pltpu.VMEM((n_chunks, SSD_GROUPS, SSD_STATE, GROUP_W), BF16),
            pltpu.VMEM((n_chunks, CHUNK, LANES), F32),
            pltpu.VMEM((n_chunks, N_DIR * SSD_HEADS, CHUNK), F32),
            pltpu.VMEM((n_chunks, CHUNK, LANES), BF16),
            pltpu.VMEM((n_chunks, CHUNK, LANES), BF16),
            pltpu.VMEM((n_chunks, 8, LANES), F32),
            pltpu.VMEM((SSD_GROUPS, SSD_STATE, GROUP_W), F32),
            pltpu.VMEM((SSD_GROUPS, SSD_STATE, GROUP_W), F32),
        ],
        compiler_params=pltpu.CompilerParams(
            dimension_semantics=("arbitrary", "arbitrary"), vmem_limit_bytes=VMEM_LIMIT),
        name="ssd",
    )(xbc, xbc, xbc, zs, dtr, conv_w8, conv_b, dt_bias128, a128, dskip512, ssd_norm_w,
      jnp.asarray(shift, BF16), jnp.asarray(e_f, BF16), jnp.asarray(e_b, BF16))


def _mix_kernel(x_ref, q_ref, kl_ref, km_ref, kr_ref, vl_ref, vm_ref, vr_ref, ga_ref,
                qm_ref, gm_ref, mk_ref, mv_ref, ssd_ref, sink_ref, wo_ref, now_ref,
                o_ref, kst_ref, vst_ref, p_ref, att_ref, xat_ref, *, n_tiles):
    i = pl.program_id(1)
    tq = q_ref.shape[1]
    nblk = tq // BLOCK
    span = BLOCK + 2 * WINDOW

    rr = lax.broadcasted_iota(jnp.int32, (BLOCK, WINDOW), 0)
    cc = lax.broadcasted_iota(jnp.int32, (BLOCK, WINDOW), 1)
    left_bias0 = jnp.where(cc >= rr, 0.0, NEG)
    right_bias0 = jnp.where(cc <= rr, 0.0, NEG)
    slot = lax.broadcasted_iota(jnp.int32, (span, GROUP_W), 1) % LANES // HALF
    vslot = lax.broadcasted_iota(jnp.int32, (span, GROUP_W), 1) // ATT_HEAD_DIM
    oslot = lax.broadcasted_iota(jnp.int32, (BLOCK, GROUP_W), 1) // ATT_HEAD_DIM
    kmask = [jnp.where(slot == j, 1.0, 0.0).astype(BF16) for j in range(4)]
    vmask = [jnp.where(vslot == j, 1.0, 0.0).astype(BF16) for j in range(4)]

    for blk in range(nblk):
        r0 = blk * BLOCK
        def piece(l_ref, m_ref, r_ref, which, g):
            lo = GROUP_W * g
            if which == 0:
                if blk == 0:
                    return l_ref[0, :, lo:lo + GROUP_W]
                return m_ref[0, r0 - BLOCK:r0, lo:lo + GROUP_W]
            if which == 1:
                return m_ref[0, r0:r0 + BLOCK, lo:lo + GROUP_W]
            if blk == nblk - 1:
                return r_ref[0, :, lo:lo + GROUP_W]
            return m_ref[0, r0 + BLOCK:r0 + 2 * BLOCK, lo:lo + GROUP_W]

        left_dead = jnp.logical_and(i == 0, blk == 0)
        right_dead = jnp.logical_and(i == n_tiles - 1, blk == nblk - 1)
        left_bias = left_bias0 + jnp.where(left_dead, NEG, 0.0)
        right_bias = right_bias0 + jnp.where(right_dead, NEG, 0.0)

        for g in range(ATT_KV_HEADS):
            kwin = jnp.concatenate([piece(kl_ref, km_ref, kr_ref, w, g) for w in range(3)], axis=0)
            vwin = jnp.concatenate([piece(vl_ref, vm_ref, vr_ref, w, g) for w in range(3)], axis=0)
            for j in range(4):
                kst_ref[span * j:span * (j + 1), :] = kwin * kmask[j]
                vst_ref[span * j:span * (j + 1), :] = vwin * vmask[j]
            qg = q_ref[0, r0:r0 + BLOCK, GROUP_W * g:GROUP_W * (g + 1)]
            s_cat = _dot_nt(qg, kst_ref[...])
            inv = jnp.zeros((BLOCK, GROUP_W), F32)
            for j in range(4):
                h = 4 * g + j
                s_l = s_cat[:, span * j:span * j + WINDOW] + left_bias
                s_m = s_cat[:, span * j + WINDOW:span * j + WINDOW + BLOCK]
                s_r = s_cat[:, span * j + WINDOW + BLOCK:span * (j + 1)] + right_bias
                sink = sink_ref[0, h]
                m = jnp.maximum(jnp.max(jnp.maximum(jnp.maximum(s_l, s_m), s_r), axis=-1, keepdims=True), sink)
                p_l = jnp.exp(s_l - m)
                p_m = jnp.exp(s_m - m)
                p_r = jnp.exp(s_r - m)
                den = jnp.sum(p_l + p_m + p_r, axis=-1, keepdims=True) + jnp.exp(sink - m)
                p_ref[:, span * j:span * j + WINDOW] = p_l.astype(BF16)
                p_ref[:, span * j + WINDOW:span * j + WINDOW + BLOCK] = p_m.astype(BF16)
                p_ref[:, span * j + WINDOW + BLOCK:span * (j + 1)] = p_r.astype(BF16)
                inv = jnp.where(oslot == j, jnp.broadcast_to(1.0 / den, inv.shape), inv)
            o = _dot(p_ref[...], vst_ref[...]) * inv
            gate = ga_ref[0, r0:r0 + BLOCK, GROUP_W * g:GROUP_W * (g + 1)].astype(F32)
            att_ref[r0:r0 + BLOCK, GROUP_W * g:GROUP_W * (g + 1)] = (o * gate).astype(BF16)

    for h in range(MEM_HEADS):
        lo = MEM_HEAD_DIM * h
        s = _dot_nt(qm_ref[0, :, lo:lo + MEM_HEAD_DIM], mk_ref[0, :, lo:lo + MEM_HEAD_DIM])
        m = jnp.max(s, axis=-1, keepdims=True)
        p = jnp.exp(s - m)
        den = jnp.sum(p, axis=-1, keepdims=True)
        o = _dot(p.astype(BF16), mv_ref[0, :, lo:lo + MEM_HEAD_DIM]) * (1.0 / den)
        xat_ref[:, lo:lo + MEM_HEAD_DIM] = (o * gm_ref[0, :, lo:lo + MEM_HEAD_DIM].astype(F32)).astype(BF16)

    delta = _dot(att_ref[...], wo_ref[0:ATT_W, :])
    delta = delta + _dot(ssd_ref[0], wo_ref[ATT_W:ATT_W + SSD_W, :])
    delta = delta + _dot(xat_ref[...], wo_ref[ATT_W + SSD_W:, :])
    hres = x_ref[0] + delta
    ms = jnp.mean(hres * hres, axis=-1, keepdims=True)
    o_ref[0] = hres * lax.rsqrt(ms + EPS) * now_ref[...]


def _mix(x, q, k, v, ga, qm, gm, mk, mv, ssd, sink, w_out_bf, norm_out_w):
    b, seq_len, _ = x.shape
    tq = MIX_TQ
    n_tiles = seq_len // tq
    bpt = tq // BLOCK
    n_blocks = seq_len // BLOCK
    span = BLOCK + 2 * WINDOW
    main = lambda w: pl.BlockSpec((1, tq, w), lambda bi, i: (bi, i, 0))
    left = lambda w: pl.BlockSpec((1, BLOCK, w), lambda bi, i: (bi, jnp.maximum(i * bpt - 1, 0), 0))
    right = lambda w: pl.BlockSpec((1, BLOCK, w), lambda bi, i: (bi, jnp.minimum((i + 1) * bpt, n_blocks - 1), 0))
    per_b = lambda r, w: pl.BlockSpec((1, r, w), lambda bi, i: (bi, 0, 0))
    const2 = lambda bi, i: (0, 0)
    in_specs = [
        main(D_MODEL), main(ATT_W),
        left(ATT_W), main(ATT_W), right(ATT_W),
        left(ATT_W), main(ATT_W), right(ATT_W),
        main(ATT_W), main(MEM_W), main(MEM_W),
        per_b(MEM_LEN, MEM_W), per_b(MEM_LEN, MEM_W),
        main(SSD_W),
        pl.BlockSpec(memory_space=pltpu.SMEM),
        pl.BlockSpec((MIX_W, D_MODEL), const2),
        pl.BlockSpec((1, D_MODEL), const2),
    ]
    return pl.pallas_call(
        functools.partial(_mix_kernel, n_tiles=n_tiles),
        grid=(b, n_tiles),
        in_specs=in_specs,
        out_specs=pl.BlockSpec((1, tq, D_MODEL), lambda bi, i: (bi, i, 0)),
        out_shape=jax.ShapeDtypeStruct((b, seq_len, D_MODEL), F32),
        scratch_shapes=[
            pltpu.VMEM((4 * span, GROUP_W), BF16),
            pltpu.VMEM((4 * span, GROUP_W), BF16),
            pltpu.VMEM((BLOCK, 4 * span), BF16),
            pltpu.VMEM((tq, ATT_W), BF16),
            pltpu.VMEM((tq, MEM_W), BF16),
        ],
        compiler_params=pltpu.CompilerParams(
            dimension_semantics=("arbitrary", "arbitrary"), vmem_limit_bytes=VMEM_LIMIT),
        name="mix",
    )(x, q, k, k, k, v, v, v, ga, qm, gm, mk, mv, ssd, sink, w_out_bf, norm_out_w.reshape(1, D_MODEL))


def _q_perm():
    idx = np.zeros((ATT_W,), np.int32)
    for g in range(ATT_KV_HEADS):
        for half in range(2):
            for j in range(4):
                for d in range(HALF):
                    idx[GROUP_W * g + LANES * half + HALF * j + d] = (4 * g + j) * ATT_HEAD_DIM + HALF * half + d
    return idx


def _k_perm():
    idx = np.zeros((KV_W,), np.int32)
    for half in range(2):
        for g in range(ATT_KV_HEADS):
            for d in range(HALF):
                idx[LANES * half + HALF * g + d] = g * ATT_HEAD_DIM + HALF * half + d
    return idx


def kernel(x, mem, norm_mem_w, norm_in_w, w_in, attn_sink, conv_w, conv_b, dt_bias, a_log, d_skip,
           ssd_norm_w, w_mem_kv, w_out, norm_out_w):
    b, seq_len, _ = x.shape
    depth = w_in.shape[0]

    pos = jnp.arange(seq_len, dtype=F32)
    inv_freq = 1.0 / (ROPE_THETA ** (jnp.arange(0, ATT_HEAD_DIM, 2, dtype=F32) / ATT_HEAD_DIM))
    ang = pos[:, None] * inv_freq[None, :]
    cos128 = jnp.tile(jnp.cos(ang), (1, 4))
    sin128 = jnp.tile(jnp.sin(ang), (1, 4))

    assert depth == 1, "the final norm is fused into the single layer's output stage"
    layer = 0
    w = w_in[layer]
    o = 0
    parts = []
    for width in (ATT_W, KV_W, KV_W, ATT_W, SSD_W, XBC_W, N_DIR * SSD_HEADS, MEM_W, MEM_W):
        parts.append(w[:, o:o + width])
        o += width
    wq, wk, wv, wga, wz, wxbc, wdt, wqm, wgm = parts
    wq = wq[:, _q_perm()]
    wk = wk[:, _k_perm()]
    pad16 = LANES - N_DIR * SSD_HEADS
    wdt = jnp.pad(wdt, ((0, 0), (0, pad16)))
    weights = [m.astype(BF16) for m in (wq, wk, wv, wga, wz, wxbc, wdt, wqm, wgm)]

    mk, mv = _memkv(mem, norm_mem_w, w_mem_kv[layer].astype(BF16))
    q, k, v, ga, zs, xbc, dtr, qm, gm = _proj(
        x.reshape(b * seq_len, D_MODEL), norm_in_w[layer], cos128, sin128, weights, seq_len)

    dt_bias128 = jnp.pad(dt_bias[layer].reshape(1, -1).astype(F32), ((0, 0), (0, pad16)))
    a128 = jnp.pad(-jnp.exp(a_log[layer].astype(F32)).reshape(1, -1), ((0, 0), (0, pad16)))
    dskip512 = jnp.repeat(d_skip[layer].astype(F32), SSD_HEAD_DIM).reshape(1, SSD_W)
    conv_w8 = jnp.pad(conv_w[layer].astype(F32), ((0, 8 - SSD_CONV), (0, 0)))
    ssd = _ssd(xbc.reshape(b, seq_len, XBC_W), zs.reshape(b, seq_len, SSD_W), dtr,
               conv_w8, conv_b[layer].reshape(1, XBC_W).astype(F32), dt_bias128, a128, dskip512,
               ssd_norm_w[layer].reshape(1, SSD_W).astype(F32))

    r3 = lambda arr: arr.reshape(b, seq_len, arr.shape[-1])
    return _mix(x, r3(q), r3(k), r3(v), r3(ga), r3(qm), r3(gm), mk, mv, ssd,
                attn_sink[layer].reshape(1, ATT_HEADS).astype(F32), w_out[layer].astype(BF16), norm_out_w)
```

```python
import functools
import math

import jax
import jax.numpy as jnp
import numpy as np
from jax import lax
from jax.experimental import pallas as pl
from jax.experimental.pallas import tpu as pltpu

F32 = jnp.float32
BF16 = jnp.bfloat16

D_MODEL = 1024
ATT_HEADS = 16
ATT_KV_HEADS = 4
ATT_HEAD_DIM = 64
HALF = ATT_HEAD_DIM // 2
ATT_W = ATT_HEADS * ATT_HEAD_DIM
KV_W = ATT_KV_HEADS * ATT_HEAD_DIM
WINDOW = 128
BLOCK = 128
ROPE_THETA = 10000.0
SSD_HEADS = 8
SSD_HEAD_DIM = 64
SSD_W = SSD_HEADS * SSD_HEAD_DIM
SSD_GROUPS = 2
SSD_HPG = SSD_HEADS // SSD_GROUPS
SSD_STATE = 128
SSD_CONV = 5
CHUNK = 128
N_DIR = 2
XBC_W = SSD_W + 2 * SSD_GROUPS * SSD_STATE
MEM_LEN = 256
MEM_HEADS = 4
MEM_HEAD_DIM = 128
MEM_W = MEM_HEADS * MEM_HEAD_DIM
MIX_W = ATT_W + SSD_W + MEM_W
EPS = 1e-6

LANES = 128
GROUP_W = 256
HALO = 16
NEG = -1e30
VMEM_LIMIT = 56 * 1024 * 1024
LOG2E = math.log2(math.e)

PROJ_TM = 256
MIX_TQ = 256
SSD_CPS = 4


def _silu(v):
    return v * (1.0 / (1.0 + jnp.exp(-v)))


def _softplus(v):
    return jnp.maximum(v, 0.0) + jnp.log(1.0 + jnp.exp(-jnp.abs(v)))


def _dot(a, b):
    return jnp.dot(a, b, preferred_element_type=F32)


def _dot_nt(a, b):
    return lax.dot_general(a, b, (((1,), (1,)), ((), ())), preferred_element_type=F32)


def _split3(v):
    hi = v.astype(BF16)
    r1 = v - hi.astype(F32)
    mid = r1.astype(BF16)
    lo = (r1 - mid.astype(F32)).astype(BF16)
    return hi, mid, lo


def _memkv_kernel(mem_ref, nw_ref, w_ref, mk_ref, mv_ref):
    m = mem_ref[0]
    ms = jnp.mean(m * m, axis=-1, keepdims=True)
    mn = (m * lax.rsqrt(ms + EPS) * nw_ref[...]).astype(BF16)
    mk_ref[0] = _dot(mn, w_ref[:, :MEM_W]).astype(BF16)
    mv_ref[0] = _dot(mn, w_ref[:, MEM_W:]).astype(BF16)


def _memkv(mem, norm_mem_w, w_mem_kv_bf):
    b = mem.shape[0]
    return pl.pallas_call(
        _memkv_kernel,
        grid=(b,),
        in_specs=[
            pl.BlockSpec((1, MEM_LEN, D_MODEL), lambda i: (i, 0, 0)),
            pl.BlockSpec((1, D_MODEL), lambda i: (0, 0)),
            pl.BlockSpec((D_MODEL, 2 * MEM_W), lambda i: (0, 0)),
        ],
        out_specs=[
            pl.BlockSpec((1, MEM_LEN, MEM_W), lambda i: (i, 0, 0)),
            pl.BlockSpec((1, MEM_LEN, MEM_W), lambda i: (i, 0, 0)),
        ],
        out_shape=[jax.ShapeDtypeStruct((b, MEM_LEN, MEM_W), BF16)] * 2,
        compiler_params=pltpu.CompilerParams(
            dimension_semantics=("arbitrary",), vmem_limit_bytes=VMEM_LIMIT),
        name="memkv",
    )(mem, norm_mem_w.reshape(1, D_MODEL), w_mem_kv_bf)


def _proj_kernel(x_ref, nw_ref, cos_ref, sin_ref,
                 wq_ref, wk_ref, wv_ref, wga_ref, wz_ref, wxbc_ref, wdt_ref, wqm_ref, wgm_ref,
                 q_out, k_out, v_out, ga_out, z_out, xbc_out, dt_out, qm_out, gm_out):
    x = x_ref[...]
    ms = jnp.mean(x * x, axis=-1, keepdims=True)
    hn = (x * lax.rsqrt(ms + EPS) * nw_ref[...]).astype(BF16)
    c = cos_ref[...]
    s = sin_ref[...]
    att_scale = ATT_HEAD_DIM ** -0.5 * LOG2E
    mem_scale = MEM_HEAD_DIM ** -0.5 * LOG2E

    for g in range(ATT_KV_HEADS):
        t = _dot(hn, wq_ref[:, GROUP_W * g:GROUP_W * (g + 1)])
        t1 = t[:, :LANES]
        t2 = t[:, LANES:]
        q_out[:, GROUP_W * g:GROUP_W * g + LANES] = ((t1 * c - t2 * s) * att_scale).astype(BF16)
        q_out[:, GROUP_W * g + LANES:GROUP_W * (g + 1)] = ((t1 * s + t2 * c) * att_scale).astype(BF16)

    t = _dot(hn, wk_ref[...])
    t1 = t[:, :LANES]
    t2 = t[:, LANES:]
    ko = (t1 * c - t2 * s, t1 * s + t2 * c)
    slot32 = lax.broadcasted_iota(jnp.int32, t1.shape, 1) // HALF
    for half in range(2):
        r = [ko[half]] + [pltpu.roll(ko[half], HALF * m, 1) for m in range(1, 4)]
        for g in range(ATT_KV_HEADS):
            rep = r[(3 - g) % 4]
            for j in (2, 1, 0):
                rep = jnp.where(slot32 == j, r[(j - g) % 4], rep)
            k_out[:, GROUP_W * g + LANES * half:GROUP_W * g + LANES * (half + 1)] = rep.astype(BF16)

    t = _dot(hn, wv_ref[...])
    low = lax.broadcasted_iota(jnp.int32, (t.shape[0], LANES), 1) < ATT_HEAD_DIM
    for pair in range(2):
        a = t[:, LANES * pair:LANES * (pair + 1)]
        ra = pltpu.roll(a, ATT_HEAD_DIM, 1)
        even = jnp.where(low, a, ra).astype(BF16)
        odd = jnp.where(low, ra, a).astype(BF16)
        for rep in range(2):
            v_out[:, GROUP_W * (2 * pair) + LANES * rep:GROUP_W * (2 * pair) + LANES * (rep + 1)] = even
            v_out[:, GROUP_W * (2 * pair + 1) + LANES * rep:GROUP_W * (2 * pair + 1) + LANES * (rep + 1)] = odd

    for j in range(ATT_W // GROUP_W):
        t = _dot(hn, wga_ref[:, GROUP_W * j:GROUP_W * (j + 1)])
        ga_out[:, GROUP_W * j:GROUP_W * (j + 1)] = _silu(t).astype(BF16)
    for j in range(SSD_W // GROUP_W):
        t = _dot(hn, wz_ref[:, GROUP_W * j:GROUP_W * (j + 1)])
        z_out[:, GROUP_W * j:GROUP_W * (j + 1)] = _silu(t).astype(BF16)
    for j in range(XBC_W // GROUP_W):
        t = _dot(hn, wxbc_ref[:, GROUP_W * j:GROUP_W * (j + 1)])
        xbc_out[:, GROUP_W * j:GROUP_W * (j + 1)] = t.astype(BF16)
    dt_out[...] = _dot(hn, wdt_ref[...])
    for j in range(MEM_W // GROUP_W):
        t = _dot(hn, wqm_ref[:, GROUP_W * j:GROUP_W * (j + 1)])
        qm_out[:, GROUP_W * j:GROUP_W * (j + 1)] = (t * mem_scale).astype(BF16)
        t = _dot(hn, wgm_ref[:, GROUP_W * j:GROUP_W * (j + 1)])
        gm_out[:, GROUP_W * j:GROUP_W * (j + 1)] = _silu(t).astype(BF16)


def _proj(x2, norm_w, cos128, sin128, weights, seq_len):
    t_rows = x2.shape[0]
    tm = PROJ_TM
    tiles_per_seq = seq_len // tm
    widths = (ATT_W, ATT_W, ATT_W, ATT_W, SSD_W, XBC_W, LANES, MEM_W, MEM_W)
    dtypes = (BF16, BF16, BF16, BF16, BF16, BF16, F32, BF16, BF16)
    row = lambda i: (i, 0)
    const = lambda i: (0, 0)
    in_specs = [
        pl.BlockSpec((tm, D_MODEL), row),
        pl.BlockSpec((1, D_MODEL), const),
        pl.BlockSpec((tm, LANES), lambda i: (i % tiles_per_seq, 0)),
        pl.BlockSpec((tm, LANES), lambda i: (i % tiles_per_seq, 0)),
    ] + [pl.BlockSpec(w.shape, const) for w in weights]
    return pl.pallas_call(
        _proj_kernel,
        grid=(t_rows // tm,),
        in_specs=in_specs,
        out_specs=[pl.BlockSpec((tm, w), row) for w in widths],
        out_shape=[jax.ShapeDtypeStruct((t_rows, w), d) for w, d in zip(widths, dtypes)],
        compiler_params=pltpu.CompilerParams(
            dimension_semantics=("arbitrary",), vmem_limit_bytes=VMEM_LIMIT),
        name="proj",
    )(x2, norm_w.reshape(1, D_MODEL), cos128, sin128, *weights)


def _ssd_kernel(xl_ref, xm_ref, xr_ref, z_ref, dt_ref, cw_ref, cb_ref, dtb_ref, a_ref, dsk_ref, nw_ref,
                shift_ref, ef_ref, eb_ref,
                y_ref,
                ext_ref, xc_all, gb_all, z_all, rt_all, rs_all, sw_all, tl_all, hf_ref, gb_ref, *, n_steps):
    t = pl.program_id(1)
    bwd = t < n_steps
    cps = xm_ref.shape[1] // CHUNK

    tl_rows = tl_all.shape[1]

    def expand(mats, tl_part, e_ref):
        lhs = jnp.concatenate(list(mats) + list(_split3(tl_part)), axis=0)
        r = _dot(lhs, e_ref[...])
        n = len(mats) * CHUNK
        outs = [r[CHUNK * m:CHUNK * (m + 1), :] for m in range(len(mats))]
        dec = jnp.exp2(r[n:n + tl_rows, :] + r[n + tl_rows:n + 2 * tl_rows, :] + r[n + 2 * tl_rows:n + 3 * tl_rows, :])
        return outs, dec

    def state_update(st_ref, xc, wexp, dec):
        for g in range(SSD_GROUPS):
            lo = GROUP_W * g
            xs = xc[:, lo:lo + GROUP_W]
            bm = xc[:, SSD_W + SSD_STATE * g:SSD_W + SSD_STATE * (g + 1)]
            xw = (xs * wexp[:, lo:lo + GROUP_W]).astype(BF16)
            upd = _dot(bm.T.astype(BF16), xw)
            st_ref[g] = st_ref[g] * dec[0:1, lo:lo + GROUP_W] + upd

    def decay_tables(c, k):
        rr = lax.broadcasted_iota(jnp.int32, (CHUNK, CHUNK), 0)
        cc = lax.broadcasted_iota(jnp.int32, (CHUNK, CHUNK), 1)
        fwd_col = cc < SSD_HEADS
        dt = _softplus(dt_ref[CHUNK * k:CHUNK * (k + 1), :] + dtb_ref[...])
        adt = dt * (a_ref[...] * LOG2E)
        cum = jnp.where(rr >= cc, 1.0, 0.0).astype(BF16)
        cum = jnp.concatenate([cum, jnp.ones((CHUNK, CHUNK), BF16)], axis=0)
        r3 = _dot(cum, jnp.concatenate(_split3(adt), axis=1))
        r3 = r3[:, 0:LANES] + r3[:, LANES:2 * LANES] + r3[:, 2 * LANES:]
        cs = r3[0:CHUNK, :]
        tl = r3[CHUNK:, :]
        zl = jnp.where(fwd_col, cs, cs - adt)
        ldt = jnp.log2(dt)
        rt = jnp.where(fwd_col, zl - ldt, zl + ldt).T
        rs = jnp.exp2(jnp.where(fwd_col, zl, tl - zl)).astype(BF16)
        sw = (jnp.exp2(jnp.where(fwd_col, tl - zl, zl)) * dt).astype(BF16)
        z_all[c] = zl
        rt_all[c] = rt[0:N_DIR * SSD_HEADS, :]
        rs_all[c] = rs
        sw_all[c] = sw
        tl_all[c] = tl[0:tl_rows, :]
        return sw, tl[0:tl_rows, :]

    def conv_silu(c, k):
        ext = ext_ref[CHUNK * k:CHUNK * (k + 2), :]
        pad = (SSD_CONV - 1) // 2
        acc = cb_ref[...] + cw_ref[pad:pad + 1, :] * xm_ref[0, CHUNK * k:CHUNK * (k + 1), :].astype(F32)
        shifted = _dot(shift_ref[...], ext)
        tap = 0
        for j in range(SSD_CONV):
            if j == pad:
                continue
            acc = acc + cw_ref[j:j + 1, :] * shifted[CHUNK * tap:CHUNK * (tap + 1), :]
            tap += 1
        xc = _silu(acc)
        xc_all[c] = xc.astype(BF16)
        return xc

    @pl.when(bwd)
    def _():
        step = n_steps - 1 - t

        @pl.when(t == 0)
        def _():
            gb_ref[...] = jnp.zeros_like(gb_ref)
            tail = CHUNK * cps + 2 * HALO
            ext_ref[tail:, :] = jnp.zeros((ext_ref.shape[0] - tail, XBC_W), BF16)

        lflag = jnp.where(step == 0, 0.0, 1.0)
        rflag = jnp.where(step == n_steps - 1, 0.0, 1.0)
        ext_ref[0:HALO, :] = (xl_ref[0].astype(F32) * lflag).astype(BF16)
        ext_ref[HALO:HALO + CHUNK * cps, :] = xm_ref[0]
        ext_ref[HALO + CHUNK * cps:2 * HALO + CHUNK * cps, :] = (xr_ref[0].astype(F32) * rflag).astype(BF16)
        order = list(reversed(range(cps)))
        tables = {k: decay_tables(step * cps + k, k) for k in order}
        xcs = {k: conv_silu(step * cps + k, k) for k in order}
        for k in order:
            sw, tl_part = tables[k]
            gb_all[step * cps + k] = gb_ref[...].astype(BF16)
            (wexp,), dec = expand([sw], tl_part, eb_ref)
            state_update(gb_ref, xcs[k], wexp, dec)

    def sweep_right(c, k):
        rr = lax.broadcasted_iota(jnp.int32, (CHUNK, CHUNK), 0)
        cc = lax.broadcasted_iota(jnp.int32, (CHUNK, CHUNK), 1)
        lower = rr >= cc
        upper = rr <= cc
        slot = lax.broadcasted_iota(jnp.int32, (CHUNK, GROUP_W), 1) // SSD_HEAD_DIM
        slotmask = [jnp.where(slot == j, 1.0, 0.0).astype(BF16) for j in range(SSD_HPG)]

        xc = xc_all[c].astype(F32)
        zl = z_all[c]
        rt = rt_all[c]
        rs = rs_all[c]
        scale_f = _dot(rs, ef_ref[...])
        scale_b = _dot(rs, eb_ref[...])
        for g in range(SSD_GROUPS):
            lo = GROUP_W * g
            xs = xc[:, lo:lo + GROUP_W]
            bm = xc[:, SSD_W + SSD_STATE * g:SSD_W + SSD_STATE * (g + 1)].astype(BF16)
            cm = xc[:, SSD_W + SSD_GROUPS * SSD_STATE + SSD_STATE * g:
                    SSD_W + SSD_GROUPS * SSD_STATE + SSD_STATE * (g + 1)].astype(BF16)
            gmat = _dot_nt(cm, bm)
            xs_bf = xs.astype(BF16)
            m_parts = []
            x_parts = []
            for j in range(SSD_HPG):
                hf = SSD_HPG * g + j
                hb = SSD_HEADS + hf
                w_f = jnp.where(lower, jnp.exp2(zl[:, hf:hf + 1] - rt[hf:hf + 1, :]), 0.0)
                w_b = jnp.where(upper, jnp.exp2(rt[hb:hb + 1, :] - zl[:, hb:hb + 1]), 0.0)
                m_parts.append((gmat * (w_f + w_b)).astype(BF16))
                x_parts.append(xs_bf * slotmask[j])
            m_cat = jnp.concatenate(m_parts, axis=1)
            x_stack = jnp.concatenate(x_parts, axis=0)
            y = _dot(m_cat, x_stack)
            y = y + _dot(cm, hf_ref[g].astype(BF16)) * scale_f[:, lo:lo + GROUP_W]
            y = y + _dot(cm, gb_all[c, g]) * scale_b[:, lo:lo + GROUP_W]
            y = y + dsk_ref[:, lo:lo + GROUP_W] * xs
            gated = y * z_ref[0, CHUNK * k:CHUNK * (k + 1), lo:lo + GROUP_W].astype(F32)
            ms = jnp.mean(gated * gated, axis=-1, keepdims=True)
            y_ref[0, CHUNK * k:CHUNK * (k + 1), lo:lo + GROUP_W] = (
                gated * lax.rsqrt(ms + EPS) * nw_ref[:, lo:lo + GROUP_W]).astype(BF16)
        (wexp,), dec = expand([sw_all[c]], tl_all[c], ef_ref)
        state_update(hf_ref, xc, wexp, dec)

    @pl.when(jnp.logical_not(bwd))
    def _():
        step = t - n_steps

        @pl.when(t == n_steps)
        def _():
            hf_ref[...] = jnp.zeros_like(hf_ref)

        for k in range(cps):
            sweep_right(step * cps + k, k)


def _ssd(xbc, zs, dtr, conv_w8, conv_b, dt_bias128, a128, dskip512, ssd_norm_w):
    b, seq_len, _ = xbc.shape
    n_chunks = seq_len // CHUNK
    cps = SSD_CPS
    rows_per_step = CHUNK * cps
    n_steps = n_chunks // cps
    halo_per_step = rows_per_step // HALO
    n_halo = seq_len // HALO
    win_rows = 2 * CHUNK
    ext_rows = CHUNK * (cps + 1)
    pad = (SSD_CONV - 1) // 2

    rows = np.arange(CHUNK)[:, None]
    cols = np.arange(win_rows)[None, :]
    shift = np.concatenate([(cols == rows + HALO - pad + j) for j in range(SSD_CONV) if j != pad]).astype(np.float32)
    hcol = np.arange(LANES)[:, None]
    hslot = (np.arange(SSD_W) // SSD_HEAD_DIM)[None, :]
    e_f = (hcol == hslot).astype(np.float32)
    e_b = (hcol == hslot + SSD_HEADS).astype(np.float32)

    def fwd_step(t):
        return jnp.maximum(t - n_steps, 0)

    def raw_step(t):
        return jnp.where(t < n_steps, n_steps - 1 - t, 0)

    const2 = lambda i, t: (0, 0)
    in_specs = [
        pl.BlockSpec((1, HALO, XBC_W), lambda i, t: (i, jnp.maximum(raw_step(t) * halo_per_step - 1, 0), 0)),
        pl.BlockSpec((1, rows_per_step, XBC_W), lambda i, t: (i, raw_step(t), 0)),
        pl.BlockSpec((1, HALO, XBC_W),
                     lambda i, t: (i, jnp.minimum((raw_step(t) + 1) * halo_per_step, n_halo - 1), 0)),
        pl.BlockSpec((1, rows_per_step, SSD_W), lambda i, t: (i, fwd_step(t), 0)),
        pl.BlockSpec((rows_per_step, LANES), lambda i, t: (i * n_steps + raw_step(t), 0)),
        pl.BlockSpec((8, XBC_W), const2),
        pl.BlockSpec((1, XBC_W), const2),
        pl.BlockSpec((1, LANES), const2),
        pl.BlockSpec((1, LANES), const2),
        pl.BlockSpec((1, SSD_W), const2),
        pl.BlockSpec((1, SSD_W), const2),
        pl.BlockSpec(((SSD_CONV - 1) * CHUNK, win_rows), const2),
        pl.BlockSpec((LANES, SSD_W), const2),
        pl.BlockSpec((LANES, SSD_W), const2),
    ]
    return pl.pallas_call(
        functools.partial(_ssd_kernel, n_steps=n_steps),
        grid=(b, 2 * n_steps),
        in_specs=in_specs,
        out_specs=pl.BlockSpec((1, rows_per_step, SSD_W), lambda i, t: (i, fwd_step(t), 0)),
        out_shape=jax.ShapeDtypeStruct((b, seq_len, SSD_W), BF16),
        scratch_shapes=[
            pltpu.VMEM((ext_rows, XBC_W), BF16),
            pltpu.VMEM((n_chunks, CHUNK, XBC_W), BF16),
            pltpu.VMEM((n_chunks, SSD_GROUPS, SSD_STATE, GROUP_W), BF16),
            pltpu.VMEM((n_chunks, CHUNK, LANES), F32),
            pltpu.VMEM((n_chunks, N_DIR * SSD_HEADS, CHUNK), F32),
            pltpu.VMEM((n_chunks, CHUNK, LANES), BF16),
            pltpu.VMEM((n_chunks, CHUNK, LANES), BF16),
            pltpu.VMEM((n_chunks, HALO, LANES), F32),
            pltpu.VMEM((SSD_GROUPS, SSD_STATE, GROUP_W), F32),
            pltpu.VMEM((SSD_GROUPS, SSD_STATE, GROUP_W), F32),
        ],
        compiler_params=pltpu.CompilerParams(
            dimension_semantics=("arbitrary", "arbitrary"), vmem_limit_bytes=VMEM_LIMIT),
        name="ssd",
    )(xbc, xbc, xbc, zs, dtr, conv_w8, conv_b, dt_bias128, a128, dskip512, ssd_norm_w,
      jnp.asarray(shift, BF16), jnp.asarray(e_f, BF16), jnp.asarray(e_b, BF16))


def _mix_kernel(x_ref, q_ref, kl_ref, km_ref, kr_ref, vl_ref, vm_ref, vr_ref, ga_ref,
                qm_ref, gm_ref, mk_ref, mv_ref, ssd_ref, sink_ref, wo_ref, now_ref,
                o_ref, kst_ref, vst_ref, p_ref, att_ref, xat_ref, *, n_tiles):
    i = pl.program_id(1)
    tq = q_ref.shape[1]
    nblk = tq // BLOCK
    span = BLOCK + 2 * WINDOW

    rr = lax.broadcasted_iota(jnp.int32, (BLOCK, WINDOW), 0)
    cc = lax.broadcasted_iota(jnp.int32, (BLOCK, WINDOW), 1)
    left_bias0 = jnp.where(cc >= rr, 0.0, NEG)
    right_bias0 = jnp.where(cc <= rr, 0.0, NEG)
    oslot = lax.broadcasted_iota(jnp.int32, (BLOCK, GROUP_W), 1) // ATT_HEAD_DIM
    pstride = 4 * BLOCK

    slot = lax.broadcasted_iota(jnp.int32, (BLOCK, GROUP_W), 1)
    kmask = [jnp.where(slot % LANES // HALF == j, 1.0, 0.0).astype(BF16) for j in range(4)]
    vmask = [jnp.where(slot // ATT_HEAD_DIM == j, 1.0, 0.0).astype(BF16) for j in range(4)]

    def piece(l_ref, m_ref, r_ref, p, lo):
        if p == 0:
            return l_ref[0, :, lo:lo + GROUP_W]
        if p == nblk + 1:
            return r_ref[0, :, lo:lo + GROUP_W]
        return m_ref[0, (p - 1) * BLOCK:p * BLOCK, lo:lo + GROUP_W]

    def build_stacks(g):
        lo = GROUP_W * g
        for p in range(nblk + 2):
            kp = piece(kl_ref, km_ref, kr_ref, p, lo)
            vp = piece(vl_ref, vm_ref, vr_ref, p, lo)
            for j in range(4):
                rows = pl.ds(pstride * p + BLOCK * j, BLOCK)
                kst_ref[g, rows, :] = kp * kmask[j]
                vst_ref[g, rows, :] = vp * vmask[j]

    def scores(g):
        return _dot_nt(q_ref[0, :, GROUP_W * g:GROUP_W * (g + 1)], kst_ref[g])

    def finish(g, s_all):
        lo = GROUP_W * g
        inv_rows = []
        for blk in range(nblk):
            r0 = blk * BLOCK
            left_dead = jnp.logical_and(i == 0, blk == 0)
            right_dead = jnp.logical_and(i == n_tiles - 1, blk == nblk - 1)
            left_bias = left_bias0 + jnp.where(left_dead, NEG, 0.0)
            right_bias = right_bias0 + jnp.where(right_dead, NEG, 0.0)
            inv = jnp.zeros((BLOCK, GROUP_W), F32)
            for j in range(4):
                c_l = pstride * blk + BLOCK * j
                c_m = c_l + pstride
                c_r = c_m + pstride
                s_l = s_all[r0:r0 + BLOCK, c_l:c_l + BLOCK] + left_bias
                s_m = s_all[r0:r0 + BLOCK, c_m:c_m + BLOCK]
                s_r = s_all[r0:r0 + BLOCK, c_r:c_r + BLOCK] + right_bias
                sink = sink_ref[0, 4 * g + j] * LOG2E
                m = jnp.maximum(jnp.max(jnp.maximum(jnp.maximum(s_l, s_m), s_r), axis=-1, keepdims=True), sink)
                p_l = jnp.exp2(s_l - m)
                p_m = jnp.exp2(s_m - m)
                p_r = jnp.exp2(s_r - m)
                den = jnp.sum(p_l + p_m + p_r, axis=-1, keepdims=True) + jnp.exp2(sink - m)
                p_ref[g, r0:r0 + BLOCK, c_l:c_l + BLOCK] = p_l.astype(BF16)
                p_ref[g, r0:r0 + BLOCK, c_m:c_m + BLOCK] = p_m.astype(BF16)
                p_ref[g, r0:r0 + BLOCK, c_r:c_r + BLOCK] = p_r.astype(BF16)
                inv = jnp.where(oslot == j, jnp.broadcast_to(1.0 / den, inv.shape), inv)
            inv_rows.append(inv)
        o = _dot(p_ref[g], vst_ref[g]) * jnp.concatenate(inv_rows, axis=0)
        gate = ga_ref[0, :, lo:lo + GROUP_W].astype(F32)
        att_ref[:, lo:lo + GROUP_W] = (o * gate).astype(BF16)

    @pl.when(jnp.logical_and(pl.program_id(0) == 0, i == 0))
    def _():
        for blk in range(nblk):
            for p in range(nblk + 2):
                if p < blk or p > blk + 2:
                    p_ref[:, blk * BLOCK:(blk + 1) * BLOCK, pstride * p:pstride * (p + 1)] = jnp.zeros(
                        (ATT_KV_HEADS, BLOCK, pstride), BF16)

    build_stacks(0)
    s_next = scores(0)
    for g in range(ATT_KV_HEADS):
        s_cur = s_next
        if g + 1 < ATT_KV_HEADS:
            build_stacks(g + 1)
            s_next = scores(g + 1)
        finish(g, s_cur)

    for h in range(MEM_HEADS):
        lo = MEM_HEAD_DIM * h
        s = _dot_nt(qm_ref[0, :, lo:lo + MEM_HEAD_DIM], mk_ref[0, :, lo:lo + MEM_HEAD_DIM])
        m = jnp.max(s, axis=-1, keepdims=True)
        p = jnp.exp2(s - m)
        den = jnp.sum(p, axis=-1, keepdims=True)
        o = _dot(p.astype(BF16), mv_ref[0, :, lo:lo + MEM_HEAD_DIM]) * (1.0 / den)
        xat_ref[:, lo:lo + MEM_HEAD_DIM] = (o * gm_ref[0, :, lo:lo + MEM_HEAD_DIM].astype(F32)).astype(BF16)

    delta = _dot(att_ref[...], wo_ref[0:ATT_W, :])
    delta = delta + _dot(ssd_ref[0], wo_ref[ATT_W:ATT_W + SSD_W, :])
    delta = delta + _dot(xat_ref[...], wo_ref[ATT_W + SSD_W:, :])
    hres = x_ref[0] + delta
    ms = jnp.mean(hres * hres, axis=-1, keepdims=True)
    o_ref[0] = hres * lax.rsqrt(ms + EPS) * now_ref[...]


def _mix(x, q, k, v, ga, qm, gm, mk, mv, ssd, sink, w_out_bf, norm_out_w):
    b, seq_len, _ = x.shape
    tq = MIX_TQ
    n_tiles = seq_len // tq
    bpt = tq // BLOCK
    n_blocks = seq_len // BLOCK
    span = BLOCK + 2 * WINDOW
    main = lambda w: pl.BlockSpec((1, tq, w), lambda bi, i: (bi, i, 0))
    left = lambda w: pl.BlockSpec((1, BLOCK, w), lambda bi, i: (bi, jnp.maximum(i * bpt - 1, 0), 0))
    right = lambda w: pl.BlockSpec((1, BLOCK, w), lambda bi, i: (bi, jnp.minimum((i + 1) * bpt, n_blocks - 1), 0))
    per_b = lambda r, w: pl.BlockSpec((1, r, w), lambda bi, i: (bi, 0, 0))
    const2 = lambda bi, i: (0, 0)
    in_specs = [
        main(D_MODEL), main(ATT_W),
        left(ATT_W), main(ATT_W), right(ATT_W),
        left(ATT_W), main(ATT_W), right(ATT_W),
        main(ATT_W), main(MEM_W), main(MEM_W),
        per_b(MEM_LEN, MEM_W), per_b(MEM_LEN, MEM_W),
        main(SSD_W),
        pl.BlockSpec(memory_space=pltpu.SMEM),
        pl.BlockSpec((MIX_W, D_MODEL), const2),
        pl.BlockSpec((1, D_MODEL), const2),
    ]
    return pl.pallas_call(
        functools.partial(_mix_kernel, n_tiles=n_tiles),
        grid=(b, n_tiles),
        in_specs=in_specs,
        out_specs=pl.BlockSpec((1, tq, D_MODEL), lambda bi, i: (bi, i, 0)),
        out_shape=jax.ShapeDtypeStruct((b, seq_len, D_MODEL), F32),
        scratch_shapes=[
            pltpu.VMEM((ATT_KV_HEADS, (bpt + 2) * 4 * BLOCK, GROUP_W), BF16),
            pltpu.VMEM((ATT_KV_HEADS, (bpt + 2) * 4 * BLOCK, GROUP_W), BF16),
            pltpu.VMEM((ATT_KV_HEADS, tq, (bpt + 2) * 4 * BLOCK), BF16),
            pltpu.VMEM((tq, ATT_W), BF16),
            pltpu.VMEM((tq, MEM_W), BF16),
        ],
        compiler_params=pltpu.CompilerParams(
            dimension_semantics=("arbitrary", "arbitrary"), vmem_limit_bytes=VMEM_LIMIT),
        name="mix",
    )(x, q, k, k, k, v, v, v, ga, qm, gm, mk, mv, ssd, sink, w_out_bf, norm_out_w.reshape(1, D_MODEL))


def _q_perm():
    idx = np.zeros((ATT_W,), np.int32)
    for g in range(ATT_KV_HEADS):
        for half in range(2):
            for j in range(4):
                for d in range(HALF):
                    idx[GROUP_W * g + LANES * half + HALF * j + d] = (4 * g + j) * ATT_HEAD_DIM + HALF * half + d
    return idx


def _k_perm():
    idx = np.zeros((KV_W,), np.int32)
    for half in range(2):
        for g in range(ATT_KV_HEADS):
            for d in range(HALF):
                idx[LANES * half + HALF * g + d] = g * ATT_HEAD_DIM + HALF * half + d
    return idx


def kernel(x, mem, norm_mem_w, norm_in_w, w_in, attn_sink, conv_w, conv_b, dt_bias, a_log, d_skip,
           ssd_norm_w, w_mem_kv, w_out, norm_out_w):
    b, seq_len, _ = x.shape
    depth = w_in.shape[0]

    pos = jnp.arange(seq_len, dtype=F32)
    inv_freq = 1.0 / (ROPE_THETA ** (jnp.arange(0, ATT_HEAD_DIM, 2, dtype=F32) / ATT_HEAD_DIM))
    ang = pos[:, None] * inv_freq[None, :]
    cos128 = jnp.tile(jnp.cos(ang), (1, 4))
    sin128 = jnp.tile(jnp.sin(ang), (1, 4))

    assert depth == 1, "the final norm is fused into the single layer's output stage"
    layer = 0
    w = w_in[layer]
    o = 0
    parts = []
    for width in (ATT_W, KV_W, KV_W, ATT_W, SSD_W, XBC_W, N_DIR * SSD_HEADS, MEM_W, MEM_W):
        parts.append(w[:, o:o + width])
        o += width
    wq, wk, wv, wga, wz, wxbc, wdt, wqm, wgm = parts
    wq = wq[:, _q_perm()]
    wk = wk[:, _k_perm()]
    pad16 = LANES - N_DIR * SSD_HEADS
    wdt = jnp.pad(wdt, ((0, 0), (0, pad16)))
    weights = [m.astype(BF16) for m in (wq, wk, wv, wga, wz, wxbc, wdt, wqm, wgm)]

    mk, mv = _memkv(mem, norm_mem_w, w_mem_kv[layer].astype(BF16))
    q, k, v, ga, zs, xbc, dtr, qm, gm = _proj(
        x.reshape(b * seq_len, D_MODEL), norm_in_w[layer], cos128, sin128, weights, seq_len)

    dt_bias128 = jnp.pad(dt_bias[layer].reshape(1, -1).astype(F32), ((0, 0), (0, pad16)))
    a128 = jnp.pad(-jnp.exp(a_log[layer].astype(F32)).reshape(1, -1), ((0, 0), (0, pad16)))
    dskip512 = jnp.repeat(d_skip[layer].astype(F32), SSD_HEAD_DIM).reshape(1, SSD_W)
    conv_w8 = jnp.pad(conv_w[layer].astype(F32), ((0, 8 - SSD_CONV), (0, 0)))
    ssd = _ssd(xbc.reshape(b, seq_len, XBC_W), zs.reshape(b, seq_len, SSD_W), dtr,
               conv_w8, conv_b[layer].reshape(1, XBC_W).astype(F32), dt_bias128, a128, dskip512,
               ssd_norm_w[layer].reshape(1, SSD_W).astype(F32))

    r3 = lambda arr: arr.reshape(b, seq_len, arr.shape[-1])
    return _mix(x, r3(q), r3(k), r3(v), r3(ga), r3(qm), r3(gm), mk, mv, ssd,
                attn_sink[layer].reshape(1, ATT_HEADS).astype(F32), w_out[layer].astype(BF16), norm_out_w)
```

```python
import functools
import math

import jax
import jax.numpy as jnp
import numpy as np
from jax import lax
from jax.experimental import pallas as pl
from jax.experimental.pallas import tpu as pltpu

F32 = jnp.float32
BF16 = jnp.bfloat16

D_MODEL = 1024
ATT_HEADS = 16
ATT_KV_HEADS = 4
ATT_HEAD_DIM = 64
HALF = ATT_HEAD_DIM // 2
ATT_W = ATT_HEADS * ATT_HEAD_DIM
KV_W = ATT_KV_HEADS * ATT_HEAD_DIM
WINDOW = 128
BLOCK = 128
ROPE_THETA = 10000.0
SSD_HEADS = 8
SSD_HEAD_DIM = 64
SSD_W = SSD_HEADS * SSD_HEAD_DIM
SSD_GROUPS = 2
SSD_HPG = SSD_HEADS // SSD_GROUPS
SSD_STATE = 128
SSD_CONV = 5
CHUNK = 128
N_DIR = 2
XBC_W = SSD_W + 2 * SSD_GROUPS * SSD_STATE
MEM_LEN = 256
MEM_HEADS = 4
MEM_HEAD_DIM = 128
MEM_W = MEM_HEADS * MEM_HEAD_DIM
MIX_W = ATT_W + SSD_W + MEM_W
EPS = 1e-6

LANES = 128
GROUP_W = 256
HALO = 16
NEG = -1e30
VMEM_LIMIT = 56 * 1024 * 1024
LOG2E = math.log2(math.e)

PROJ_TM = 1024
PROJ_SUB = 256
MIX_TQ = 512
MIX_SUB = 256
SSD_CPS = 4


def _silu(v):
    return v * (1.0 / (1.0 + jnp.exp(-v)))


def _softplus(v):
    return jnp.maximum(v, 0.0) + jnp.log(1.0 + jnp.exp(-jnp.abs(v)))


def _dot(a, b):
    return jnp.dot(a, b, preferred_element_type=F32)


def _dot_nt(a, b):
    return lax.dot_general(a, b, (((1,), (1,)), ((), ())), preferred_element_type=F32)


def _split3(v):
    hi = v.astype(BF16)
    r1 = v - hi.astype(F32)
    mid = r1.astype(BF16)
    lo = (r1 - mid.astype(F32)).astype(BF16)
    return hi, mid, lo


def _memkv_kernel(mem_ref, nw_ref, w_ref, mk_ref, mv_ref):
    m = mem_ref[0]
    ms = jnp.mean(m * m, axis=-1, keepdims=True)
    mn = (m * lax.rsqrt(ms + EPS) * nw_ref[...]).astype(BF16)
    mk_ref[0] = _dot(mn, w_ref[:, :MEM_W]).astype(BF16)
    mv_ref[0] = _dot(mn, w_ref[:, MEM_W:]).astype(BF16)


def _memkv(mem, norm_mem_w, w_mem_kv_bf):
    b = mem.shape[0]
    return pl.pallas_call(
        _memkv_kernel,
        grid=(b,),
        in_specs=[
            pl.BlockSpec((1, MEM_LEN, D_MODEL), lambda i: (i, 0, 0)),
            pl.BlockSpec((1, D_MODEL), lambda i: (0, 0)),
            pl.BlockSpec((D_MODEL, 2 * MEM_W), lambda i: (0, 0)),
        ],
        out_specs=[
            pl.BlockSpec((1, MEM_LEN, MEM_W), lambda i: (i, 0, 0)),
            pl.BlockSpec((1, MEM_LEN, MEM_W), lambda i: (i, 0, 0)),
        ],
        out_shape=[jax.ShapeDtypeStruct((b, MEM_LEN, MEM_W), BF16)] * 2,
        compiler_params=pltpu.CompilerParams(
            dimension_semantics=("arbitrary",), vmem_limit_bytes=VMEM_LIMIT),
        name="memkv",
    )(mem, norm_mem_w.reshape(1, D_MODEL), w_mem_kv_bf)


def _proj_kernel(x_ref, nw_ref, cos_ref, sin_ref, *refs):
    n_w = 9
    w_refs, out_refs = refs[:n_w], refs[n_w:]
    for r0 in range(0, x_ref.shape[0], PROJ_SUB):
        rows = pl.ds(r0, PROJ_SUB)
        _proj_rows(x_ref.at[rows], nw_ref, cos_ref.at[rows], sin_ref.at[rows], *w_refs,
                   *[o.at[rows] for o in out_refs])


def _proj_rows(x_ref, nw_ref, cos_ref, sin_ref,
               wq_ref, wk_ref, wv_ref, wga_ref, wz_ref, wxbc_ref, wdt_ref, wqm_ref, wgm_ref,
               q_out, k_out, v_out, ga_out, z_out, xbc_out, dt_out, qm_out, gm_out):
    x = x_ref[...]
    ms = jnp.mean(x * x, axis=-1, keepdims=True)
    hn = (x * lax.rsqrt(ms + EPS) * nw_ref[...]).astype(BF16)
    c = cos_ref[...]
    s = sin_ref[...]
    att_scale = ATT_HEAD_DIM ** -0.5 * LOG2E
    mem_scale = MEM_HEAD_DIM ** -0.5 * LOG2E

    for g in range(ATT_KV_HEADS):
        t = _dot(hn, wq_ref[:, GROUP_W * g:GROUP_W * (g + 1)])
        t1 = t[:, :LANES]
        t2 = t[:, LANES:]
        q_out[:, GROUP_W * g:GROUP_W * g + LANES] = ((t1 * c - t2 * s) * att_scale).astype(BF16)
        q_out[:, GROUP_W * g + LANES:GROUP_W * (g + 1)] = ((t1 * s + t2 * c) * att_scale).astype(BF16)

    t = _dot(hn, wk_ref[...])
    t1 = t[:, :LANES]
    t2 = t[:, LANES:]
    ko = (t1 * c - t2 * s, t1 * s + t2 * c)
    slot32 = lax.broadcasted_iota(jnp.int32, t1.shape, 1) // HALF
    for half in range(2):
        r = [ko[half]] + [pltpu.roll(ko[half], HALF * m, 1) for m in range(1, 4)]
        for g in range(ATT_KV_HEADS):
            rep = r[(3 - g) % 4]
            for j in (2, 1, 0):
                rep = jnp.where(slot32 == j, r[(j - g) % 4], rep)
            k_out[:, GROUP_W * g + LANES * half:GROUP_W * g + LANES * (half + 1)] = rep.astype(BF16)

    t = _dot(hn, wv_ref[...])
    low = lax.broadcasted_iota(jnp.int32, (t.shape[0], LANES), 1) < ATT_HEAD_DIM
    for pair in range(2):
        a = t[:, LANES * pair:LANES * (pair + 1)]
        ra = pltpu.roll(a, ATT_HEAD_DIM, 1)
        even = jnp.where(low, a, ra).astype(BF16)
        odd = jnp.where(low, ra, a).astype(BF16)
        for rep in range(2):
            v_out[:, GROUP_W * (2 * pair) + LANES * rep:GROUP_W * (2 * pair) + LANES * (rep + 1)] = even
            v_out[:, GROUP_W * (2 * pair + 1) + LANES * rep:GROUP_W * (2 * pair + 1) + LANES * (rep + 1)] = odd

    for j in range(ATT_W // GROUP_W):
        t = _dot(hn, wga_ref[:, GROUP_W * j:GROUP_W * (j + 1)])
        ga_out[:, GROUP_W * j:GROUP_W * (j + 1)] = _silu(t).astype(BF16)
    for j in range(SSD_W // GROUP_W):
        t = _dot(hn, wz_ref[:, GROUP_W * j:GROUP_W * (j + 1)])
        z_out[:, GROUP_W * j:GROUP_W * (j + 1)] = _silu(t).astype(BF16)
    for j in range(XBC_W // GROUP_W):
        t = _dot(hn, wxbc_ref[:, GROUP_W * j:GROUP_W * (j + 1)])
        xbc_out[:, GROUP_W * j:GROUP_W * (j + 1)] = t.astype(BF16)
    dt_out[...] = _dot(hn, wdt_ref[...])
    for j in range(MEM_W // GROUP_W):
        t = _dot(hn, wqm_ref[:, GROUP_W * j:GROUP_W * (j + 1)])
        qm_out[:, GROUP_W * j:GROUP_W * (j + 1)] = (t * mem_scale).astype(BF16)
        t = _dot(hn, wgm_ref[:, GROUP_W * j:GROUP_W * (j + 1)])
        gm_out[:, GROUP_W * j:GROUP_W * (j + 1)] = _silu(t).astype(BF16)


def _proj(x2, norm_w, cos128, sin128, weights, seq_len):
    t_rows = x2.shape[0]
    tm = PROJ_TM
    tiles_per_seq = seq_len // tm
    widths = (ATT_W, ATT_W, ATT_W, ATT_W, SSD_W, XBC_W, LANES, MEM_W, MEM_W)
    dtypes = (BF16, BF16, BF16, BF16, BF16, BF16, F32, BF16, BF16)
    row = lambda i: (i, 0)
    const = lambda i: (0, 0)
    in_specs = [
        pl.BlockSpec((tm, D_MODEL), row),
        pl.BlockSpec((1, D_MODEL), const),
        pl.BlockSpec((tm, LANES), lambda i: (i % tiles_per_seq, 0)),
        pl.BlockSpec((tm, LANES), lambda i: (i % tiles_per_seq, 0)),
    ] + [pl.BlockSpec(w.shape, const, pipeline_mode=pl.Buffered(1)) for w in weights]
    return pl.pallas_call(
        _proj_kernel,
        grid=(t_rows // tm,),
        in_specs=in_specs,
        out_specs=[pl.BlockSpec((tm, w), row) for w in widths],
        out_shape=[jax.ShapeDtypeStruct((t_rows, w), d) for w, d in zip(widths, dtypes)],
        compiler_params=pltpu.CompilerParams(
            dimension_semantics=("arbitrary",), vmem_limit_bytes=VMEM_LIMIT),
        name="proj",
    )(x2, norm_w.reshape(1, D_MODEL), cos128, sin128, *weights)


def _ssd_kernel(xl_ref, xm_ref, xr_ref, z_ref, dt_ref, cw_ref, cb_ref, dtb_ref, a_ref, dsk_ref, nw_ref,
                shift_ref, ef_ref, eb_ref,
                y_ref,
                ext_ref, xc_all, gb_all, z_all, rt_all, rs_all, sw_all, tl_all, hf_ref, gb_ref, *, n_steps):
    t = pl.program_id(1)
    bwd = t < n_steps
    cps = xm_ref.shape[1] // CHUNK

    tl_rows = tl_all.shape[1]

    def expand(mats, tl_part, e_ref):
        lhs = jnp.concatenate(list(mats) + list(_split3(tl_part)), axis=0)
        r = _dot(lhs, e_ref[...])
        n = len(mats) * CHUNK
        outs = [r[CHUNK * m:CHUNK * (m + 1), :] for m in range(len(mats))]
        dec = jnp.exp2(r[n:n + tl_rows, :] + r[n + tl_rows:n + 2 * tl_rows, :] + r[n + 2 * tl_rows:n + 3 * tl_rows, :])
        return outs, dec

    def state_update(st_ref, xc, wexp, dec):
        for g in range(SSD_GROUPS):
            lo = GROUP_W * g
            xs = xc[:, lo:lo + GROUP_W]
            bm = xc[:, SSD_W + SSD_STATE * g:SSD_W + SSD_STATE * (g + 1)]
            xw = (xs * wexp[:, lo:lo + GROUP_W]).astype(BF16)
            upd = _dot(bm.T.astype(BF16), xw)
            st_ref[g] = st_ref[g] * dec[0:1, lo:lo + GROUP_W] + upd

    def decay_tables(c, k):
        rr = lax.broadcasted_iota(jnp.int32, (CHUNK, CHUNK), 0)
        cc = lax.broadcasted_iota(jnp.int32, (CHUNK, CHUNK), 1)
        fwd_col = cc < SSD_HEADS
        dt = _softplus(dt_ref[CHUNK * k:CHUNK * (k + 1), :] + dtb_ref[...])
        adt = dt * (a_ref[...] * LOG2E)
        cum = jnp.where(rr >= cc, 1.0, 0.0).astype(BF16)
        cum = jnp.concatenate([cum, jnp.ones((CHUNK, CHUNK), BF16)], axis=0)
        r3 = _dot(cum, jnp.concatenate(_split3(adt), axis=1))
        r3 = r3[:, 0:LANES] + r3[:, LANES:2 * LANES] + r3[:, 2 * LANES:]
        cs = r3[0:CHUNK, :]
        tl = r3[CHUNK:, :]
        zl = jnp.where(fwd_col, cs, cs - adt)
        ldt = jnp.log2(dt)
        rt = jnp.where(fwd_col, zl - ldt, zl + ldt).T
        rs = jnp.exp2(jnp.where(fwd_col, zl, tl - zl)).astype(BF16)
        sw = (jnp.exp2(jnp.where(fwd_col, tl - zl, zl)) * dt).astype(BF16)
        z_all[c] = zl
        rt_all[c] = rt[0:N_DIR * SSD_HEADS, :]
        rs_all[c] = rs
        sw_all[c] = sw
        tl_all[c] = tl[0:tl_rows, :]
        return sw, tl[0:tl_rows, :]

    def conv_silu(c, k):
        ext = ext_ref[CHUNK * k:CHUNK * (k + 2), :]
        pad = (SSD_CONV - 1) // 2
        acc = cb_ref[...] + cw_ref[pad:pad + 1, :] * xm_ref[0, CHUNK * k:CHUNK * (k + 1), :].astype(F32)
        shifted = _dot(shift_ref[...], ext)
        tap = 0
        for j in range(SSD_CONV):
            if j == pad:
                continue
            acc = acc + cw_ref[j:j + 1, :] * shifted[CHUNK * tap:CHUNK * (tap + 1), :]
            tap += 1
        xc = _silu(acc)
        xc_all[c] = xc.astype(BF16)
        return xc

    @pl.when(bwd)
    def _():
        step = n_steps - 1 - t

        @pl.when(t == 0)
        def _():
            gb_ref[...] = jnp.zeros_like(gb_ref)
            tail = CHUNK * cps + 2 * HALO
            ext_ref[tail:, :] = jnp.zeros((ext_ref.shape[0] - tail, XBC_W), BF16)

        lflag = jnp.where(step == 0, 0.0, 1.0)
        rflag = jnp.where(step == n_steps - 1, 0.0, 1.0)
        ext_ref[0:HALO, :] = (xl_ref[0].astype(F32) * lflag).astype(BF16)
        ext_ref[HALO:HALO + CHUNK * cps, :] = xm_ref[0]
        ext_ref[HALO + CHUNK * cps:2 * HALO + CHUNK * cps, :] = (xr_ref[0].astype(F32) * rflag).astype(BF16)
        order = list(reversed(range(cps)))
        tables = {k: decay_tables(step * cps + k, k) for k in order}
        xcs = {k: conv_silu(step * cps + k, k) for k in order}
        for k in order:
            sw, tl_part = tables[k]
            gb_all[step * cps + k] = gb_ref[...].astype(BF16)
            (wexp,), dec = expand([sw], tl_part, eb_ref)
            state_update(gb_ref, xcs[k], wexp, dec)

    def sweep_right(c, k):
        rr = lax.broadcasted_iota(jnp.int32, (CHUNK, CHUNK), 0)
        cc = lax.broadcasted_iota(jnp.int32, (CHUNK, CHUNK), 1)
        lower = rr >= cc
        upper = rr <= cc
        slot = lax.broadcasted_iota(jnp.int32, (CHUNK, GROUP_W), 1) // SSD_HEAD_DIM
        slotmask = [jnp.where(slot == j, 1.0, 0.0).astype(BF16) for j in range(SSD_HPG)]

        xc = xc_all[c].astype(F32)
        zl = z_all[c]
        rt = rt_all[c]
        rs = rs_all[c]
        scale_f = _dot(rs, ef_ref[...])
        scale_b = _dot(rs, eb_ref[...])
        for g in range(SSD_GROUPS):
            lo = GROUP_W * g
            xs = xc[:, lo:lo + GROUP_W]
            bm = xc[:, SSD_W + SSD_STATE * g:SSD_W + SSD_STATE * (g + 1)].astype(BF16)
            cm = xc[:, SSD_W + SSD_GROUPS * SSD_STATE + SSD_STATE * g:
                    SSD_W + SSD_GROUPS * SSD_STATE + SSD_STATE * (g + 1)].astype(BF16)
            gmat = _dot_nt(cm, bm)
            xs_bf = xs.astype(BF16)
            m_parts = []
            x_parts = []
            for j in range(SSD_HPG):
                hf = SSD_HPG * g + j
                hb = SSD_HEADS + hf
                w_f = jnp.where(lower, jnp.exp2(zl[:, hf:hf + 1] - rt[hf:hf + 1, :]), 0.0)
                w_b = jnp.where(upper, jnp.exp2(rt[hb:hb + 1, :] - zl[:, hb:hb + 1]), 0.0)
                m_parts.append((gmat * (w_f + w_b)).astype(BF16))
                x_parts.append(xs_bf * slotmask[j])
            m_cat = jnp.concatenate(m_parts, axis=1)
            x_stack = jnp.concatenate(x_parts, axis=0)
            y = _dot(m_cat, x_stack)
            y = y + _dot(cm, hf_ref[g].astype(BF16)) * scale_f[:, lo:lo + GROUP_W]
            y = y + _dot(cm, gb_all[c, g]) * scale_b[:, lo:lo + GROUP_W]
            y = y + dsk_ref[:, lo:lo + GROUP_W] * xs
            gated = y * z_ref[0, CHUNK * k:CHUNK * (k + 1), lo:lo + GROUP_W].astype(F32)
            ms = jnp.mean(gated * gated, axis=-1, keepdims=True)
            y_ref[0, CHUNK * k:CHUNK * (k + 1), lo:lo + GROUP_W] = (
                gated * lax.rsqrt(ms + EPS) * nw_ref[:, lo:lo + GROUP_W]).astype(BF16)
        (wexp,), dec = expand([sw_all[c]], tl_all[c], ef_ref)
        state_update(hf_ref, xc, wexp, dec)

    @pl.when(jnp.logical_not(bwd))
    def _():
        step = t - n_steps

        @pl.when(t == n_steps)
        def _():
            hf_ref[...] = jnp.zeros_like(hf_ref)

        for k in range(cps):
            sweep_right(step * cps + k, k)


def _ssd(xbc, zs, dtr, conv_w8, conv_b, dt_bias128, a128, dskip512, ssd_norm_w):
    b, seq_len, _ = xbc.shape
    n_chunks = seq_len // CHUNK
    cps = SSD_CPS
    rows_per_step = CHUNK * cps
    n_steps = n_chunks // cps
    halo_per_step = rows_per_step // HALO
    n_halo = seq_len // HALO
    win_rows = 2 * CHUNK
    ext_rows = CHUNK * (cps + 1)
    pad = (SSD_CONV - 1) // 2

    rows = np.arange(CHUNK)[:, None]
    cols = np.arange(win_rows)[None, :]
    shift = np.concatenate([(cols == rows + HALO - pad + j) for j in range(SSD_CONV) if j != pad]).astype(np.float32)
    hcol = np.arange(LANES)[:, None]
    hslot = (np.arange(SSD_W) // SSD_HEAD_DIM)[None, :]
    e_f = (hcol == hslot).astype(np.float32)
    e_b = (hcol == hslot + SSD_HEADS).astype(np.float32)

    def fwd_step(t):
        return jnp.maximum(t - n_steps, 0)

    def raw_step(t):
        return jnp.where(t < n_steps, n_steps - 1 - t, 0)

    const2 = lambda i, t: (0, 0)
    in_specs = [
        pl.BlockSpec((1, HALO, XBC_W), lambda i, t: (i, jnp.maximum(raw_step(t) * halo_per_step - 1, 0), 0)),
        pl.BlockSpec((1, rows_per_step, XBC_W), lambda i, t: (i, raw_step(t), 0)),
        pl.BlockSpec((1, HALO, XBC_W),
                     lambda i, t: (i, jnp.minimum((raw_step(t) + 1) * halo_per_step, n_halo - 1), 0)),
        pl.BlockSpec((1, rows_per_step, SSD_W), lambda i, t: (i, fwd_step(t), 0)),
        pl.BlockSpec((rows_per_step, LANES), lambda i, t: (i * n_steps + raw_step(t), 0)),
        pl.BlockSpec((8, XBC_W), const2),
        pl.BlockSpec((1, XBC_W), const2),
        pl.BlockSpec((1, LANES), const2),
        pl.BlockSpec((1, LANES), const2),
        pl.BlockSpec((1, SSD_W), const2),
        pl.BlockSpec((1, SSD_W), const2),
        pl.BlockSpec(((SSD_CONV - 1) * CHUNK, win_rows), const2),
        pl.BlockSpec((LANES, SSD_W), const2),
        pl.BlockSpec((LANES, SSD_W), const2),
    ]
    return pl.pallas_call(
        functools.partial(_ssd_kernel, n_steps=n_steps),
        grid=(b, 2 * n_steps),
        in_specs=in_specs,
        out_specs=pl.BlockSpec((1, rows_per_step, SSD_W), lambda i, t: (i, fwd_step(t), 0)),
        out_shape=jax.ShapeDtypeStruct((b, seq_len, SSD_W), BF16),
        scratch_shapes=[
            pltpu.VMEM((ext_rows, XBC_W), BF16),
            pltpu.VMEM((n_chunks, CHUNK, XBC_W), BF16),
            pltpu.VMEM((n_chunks, SSD_GROUPS, SSD_STATE, GROUP_W), BF16),
            pltpu.VMEM((n_chunks, CHUNK, LANES), F32),
            pltpu.VMEM((n_chunks, N_DIR * SSD_HEADS, CHUNK), F32),
            pltpu.VMEM((n_chunks, CHUNK, LANES), BF16),
            pltpu.VMEM((n_chunks, CHUNK, LANES), BF16),
            pltpu.VMEM((n_chunks, HALO, LANES), F32),
            pltpu.VMEM((SSD_GROUPS, SSD_STATE, GROUP_W), F32),
            pltpu.VMEM((SSD_GROUPS, SSD_STATE, GROUP_W), F32),
        ],
        compiler_params=pltpu.CompilerParams(
            dimension_semantics=("arbitrary", "arbitrary"), vmem_limit_bytes=VMEM_LIMIT),
        name="ssd",
    )(xbc, xbc, xbc, zs, dtr, conv_w8, conv_b, dt_bias128, a128, dskip512, ssd_norm_w,
      jnp.asarray(shift, BF16), jnp.asarray(e_f, BF16), jnp.asarray(e_b, BF16))


def _mix_kernel(*refs, n_tiles):
    p_ref = refs[-3]
    nblk = MIX_SUB // BLOCK
    pstride = 4 * BLOCK

    @pl.when(jnp.logical_and(pl.program_id(0) == 0, pl.program_id(1) == 0))
    def _():
        for blk in range(nblk):
            for p in range(nblk + 2):
                if p < blk or p > blk + 2:
                    p_ref[:, blk * BLOCK:(blk + 1) * BLOCK, pstride * p:pstride * (p + 1)] = jnp.zeros(
                        (ATT_KV_HEADS, BLOCK, pstride), BF16)

    for sub in range(refs[1].shape[1] // MIX_SUB):
        _mix_rows(sub, *refs, n_tiles=n_tiles)


def _mix_rows(sub, x_ref, q_ref, kl_ref, km_ref, kr_ref, vl_ref, vm_ref, vr_ref, ga_ref,
              qm_ref, gm_ref, mk_ref, mv_ref, ssd_ref, sink_ref, wo_ref, now_ref,
              o_ref, kst_ref, vst_ref, p_ref, att_ref, xat_ref, *, n_tiles):
    i = pl.program_id(1)
    tq = MIX_SUB
    nsub = q_ref.shape[1] // tq
    nblk = tq // BLOCK
    rows = pl.ds(sub * tq, tq)

    rr = lax.broadcasted_iota(jnp.int32, (BLOCK, WINDOW), 0)
    cc = lax.broadcasted_iota(jnp.int32, (BLOCK, WINDOW), 1)
    left_bias0 = jnp.where(cc >= rr, 0.0, NEG)
    right_bias0 = jnp.where(cc <= rr, 0.0, NEG)
    oslot = lax.broadcasted_iota(jnp.int32, (BLOCK, GROUP_W), 1) // ATT_HEAD_DIM
    pstride = 4 * BLOCK

    slot = lax.broadcasted_iota(jnp.int32, (BLOCK, GROUP_W), 1)
    kmask = [jnp.where(slot % LANES // HALF == j, 1.0, 0.0).astype(BF16) for j in range(4)]
    vmask = [jnp.where(slot // ATT_HEAD_DIM == j, 1.0, 0.0).astype(BF16) for j in range(4)]

    def piece(l_ref, m_ref, r_ref, p, lo):
        gp = sub * nblk + p
        if gp == 0:
            return l_ref[0, :, lo:lo + GROUP_W]
        if gp == nsub * nblk + 1:
            return r_ref[0, :, lo:lo + GROUP_W]
        return m_ref[0, (gp - 1) * BLOCK:gp * BLOCK, lo:lo + GROUP_W]

    def build_stacks(g):
        lo = GROUP_W * g
        for p in range(nblk + 2):
            kp = piece(kl_ref, km_ref, kr_ref, p, lo)
            vp = piece(vl_ref, vm_ref, vr_ref, p, lo)
            for j in range(4):
                rows = pl.ds(pstride * p + BLOCK * j, BLOCK)
                kst_ref[g, rows, :] = kp * kmask[j]
                vst_ref[g, rows, :] = vp * vmask[j]

    def scores(g):
        return _dot_nt(q_ref[0, rows, GROUP_W * g:GROUP_W * (g + 1)], kst_ref[g])

    def finish(g, s_all):
        lo = GROUP_W * g
        inv_rows = []
        for blk in range(nblk):
            r0 = blk * BLOCK
            left_dead = jnp.logical_and(i == 0, sub == 0 and blk == 0)
            right_dead = jnp.logical_and(i == n_tiles - 1, sub == nsub - 1 and blk == nblk - 1)
            left_bias = left_bias0 + jnp.where(left_dead, NEG, 0.0)
            right_bias = right_bias0 + jnp.where(right_dead, NEG, 0.0)
            inv = jnp.zeros((BLOCK, GROUP_W), F32)
            for j in range(4):
                c_l = pstride * blk + BLOCK * j
                c_m = c_l + pstride
                c_r = c_m + pstride
                s_l = s_all[r0:r0 + BLOCK, c_l:c_l + BLOCK] + left_bias
                s_m = s_all[r0:r0 + BLOCK, c_m:c_m + BLOCK]
                s_r = s_all[r0:r0 + BLOCK, c_r:c_r + BLOCK] + right_bias
                sink = sink_ref[0, 4 * g + j] * LOG2E
                m = jnp.maximum(jnp.max(jnp.maximum(jnp.maximum(s_l, s_m), s_r), axis=-1, keepdims=True), sink)
                p_l = jnp.exp2(s_l - m)
                p_m = jnp.exp2(s_m - m)
                p_r = jnp.exp2(s_r - m)
                den = jnp.sum(p_l + p_m + p_r, axis=-1, keepdims=True) + jnp.exp2(sink - m)
                p_ref[g, r0:r0 + BLOCK, c_l:c_l + BLOCK] = p_l.astype(BF16)
                p_ref[g, r0:r0 + BLOCK, c_m:c_m + BLOCK] = p_m.astype(BF16)
                p_ref[g, r0:r0 + BLOCK, c_r:c_r + BLOCK] = p_r.astype(BF16)
                inv = jnp.where(oslot == j, jnp.broadcast_to(1.0 / den, inv.shape), inv)
            inv_rows.append(inv)
        o = _dot(p_ref[g], vst_ref[g]) * jnp.concatenate(inv_rows, axis=0)
        gate = ga_ref[0, rows, lo:lo + GROUP_W].astype(F32)
        att_ref[rows, lo:lo + GROUP_W] = (o * gate).astype(BF16)

    build_stacks(0)
    s_next = scores(0)
    for g in range(ATT_KV_HEADS):
        s_cur = s_next
        if g + 1 < ATT_KV_HEADS:
            build_stacks(g + 1)
            s_next = scores(g + 1)
        finish(g, s_cur)

    for h in range(MEM_HEADS):
        lo = MEM_HEAD_DIM * h
        s = _dot_nt(qm_ref[0, rows, lo:lo + MEM_HEAD_DIM], mk_ref[0, :, lo:lo + MEM_HEAD_DIM])
        m = jnp.max(s, axis=-1, keepdims=True)
        p = jnp.exp2(s - m)
        den = jnp.sum(p, axis=-1, keepdims=True)
        o = _dot(p.astype(BF16), mv_ref[0, :, lo:lo + MEM_HEAD_DIM]) * (1.0 / den)
        xat_ref[rows, lo:lo + MEM_HEAD_DIM] = (o * gm_ref[0, rows, lo:lo + MEM_HEAD_DIM].astype(F32)).astype(BF16)

    delta = _dot(att_ref[rows, :], wo_ref[0:ATT_W, :])
    delta = delta + _dot(ssd_ref[0, rows, :], wo_ref[ATT_W:ATT_W + SSD_W, :])
    delta = delta + _dot(xat_ref[rows, :], wo_ref[ATT_W + SSD_W:, :])
    hres = x_ref[0, rows, :] + delta
    ms = jnp.mean(hres * hres, axis=-1, keepdims=True)
    o_ref[0, rows, :] = hres * lax.rsqrt(ms + EPS) * now_ref[...]


def _mix(x, q, k, v, ga, qm, gm, mk, mv, ssd, sink, w_out_bf, norm_out_w):
    b, seq_len, _ = x.shape
    tq = MIX_TQ
    n_tiles = seq_len // tq
    bpt = tq // BLOCK
    sub_blocks = MIX_SUB // BLOCK
    n_blocks = seq_len // BLOCK
    span = BLOCK + 2 * WINDOW
    main = lambda w: pl.BlockSpec((1, tq, w), lambda bi, i: (bi, i, 0))
    left = lambda w: pl.BlockSpec((1, BLOCK, w), lambda bi, i: (bi, jnp.maximum(i * bpt - 1, 0), 0))
    right = lambda w: pl.BlockSpec((1, BLOCK, w), lambda bi, i: (bi, jnp.minimum((i + 1) * bpt, n_blocks - 1), 0))
    per_b = lambda r, w: pl.BlockSpec((1, r, w), lambda bi, i: (bi, 0, 0))
    const2 = lambda bi, i: (0, 0)
    in_specs = [
        main(D_MODEL), main(ATT_W),
        left(ATT_W), main(ATT_W), right(ATT_W),
        left(ATT_W), main(ATT_W), right(ATT_W),
        main(ATT_W), main(MEM_W), main(MEM_W),
        per_b(MEM_LEN, MEM_W), per_b(MEM_LEN, MEM_W),
        main(SSD_W),
        pl.BlockSpec(memory_space=pltpu.SMEM),
        pl.BlockSpec((MIX_W, D_MODEL), const2),
        pl.BlockSpec((1, D_MODEL), const2),
    ]
    return pl.pallas_call(
        functools.partial(_mix_kernel, n_tiles=n_tiles),
        grid=(b, n_tiles),
        in_specs=in_specs,
        out_specs=pl.BlockSpec((1, tq, D_MODEL), lambda bi, i: (bi, i, 0)),
        out_shape=jax.ShapeDtypeStruct((b, seq_len, D_MODEL), F32),
        scratch_shapes=[
            pltpu.VMEM((ATT_KV_HEADS, (sub_blocks + 2) * 4 * BLOCK, GROUP_W), BF16),
            pltpu.VMEM((ATT_KV_HEADS, (sub_blocks + 2) * 4 * BLOCK, GROUP_W), BF16),
            pltpu.VMEM((ATT_KV_HEADS, MIX_SUB, (sub_blocks + 2) * 4 * BLOCK), BF16),
            pltpu.VMEM((tq, ATT_W), BF16),
            pltpu.VMEM((tq, MEM_W), BF16),
        ],
        compiler_params=pltpu.CompilerParams(
            dimension_semantics=("arbitrary", "arbitrary"), vmem_limit_bytes=VMEM_LIMIT),
        name="mix",
    )(x, q, k, k, k, v, v, v, ga, qm, gm, mk, mv, ssd, sink, w_out_bf, norm_out_w.reshape(1, D_MODEL))


def _q_perm():
    idx = np.zeros((ATT_W,), np.int32)
    for g in range(ATT_KV_HEADS):
        for half in range(2):
            for j in range(4):
                for d in range(HALF):
                    idx[GROUP_W * g + LANES * half + HALF * j + d] = (4 * g + j) * ATT_HEAD_DIM + HALF * half + d
    return idx


def _k_perm():
    idx = np.zeros((KV_W,), np.int32)
    for half in range(2):
        for g in range(ATT_KV_HEADS):
            for d in range(HALF):
                idx[LANES * half + HALF * g + d] = g * ATT_HEAD_DIM + HALF * half + d
    return idx


def kernel(x, mem, norm_mem_w, norm_in_w, w_in, attn_sink, conv_w, conv_b, dt_bias, a_log, d_skip,
           ssd_norm_w, w_mem_kv, w_out, norm_out_w):
    b, seq_len, _ = x.shape
    depth = w_in.shape[0]

    pos = jnp.arange(seq_len, dtype=F32)
    inv_freq = 1.0 / (ROPE_THETA ** (jnp.arange(0, ATT_HEAD_DIM, 2, dtype=F32) / ATT_HEAD_DIM))
    ang = pos[:, None] * inv_freq[None, :]
    cos128 = jnp.tile(jnp.cos(ang), (1, 4))
    sin128 = jnp.tile(jnp.sin(ang), (1, 4))

    assert depth == 1, "the final norm is fused into the single layer's output stage"
    layer = 0
    w = w_in[layer]
    o = 0
    parts = []
    for width in (ATT_W, KV_W, KV_W, ATT_W, SSD_W, XBC_W, N_DIR * SSD_HEADS, MEM_W, MEM_W):
        parts.append(w[:, o:o + width])
        o += width
    wq, wk, wv, wga, wz, wxbc, wdt, wqm, wgm = parts
    wq = wq[:, _q_perm()]
    wk = wk[:, _k_perm()]
    pad16 = LANES - N_DIR * SSD_HEADS
    wdt = jnp.pad(wdt, ((0, 0), (0, pad16)))
    weights = [m.astype(BF16) for m in (wq, wk, wv, wga, wz, wxbc, wdt, wqm, wgm)]

    mk, mv = _memkv(mem, norm_mem_w, w_mem_kv[layer].astype(BF16))
    q, k, v, ga, zs, xbc, dtr, qm, gm = _proj(
        x.reshape(b * seq_len, D_MODEL), norm_in_w[layer], cos128, sin128, weights, seq_len)

    dt_bias128 = jnp.pad(dt_bias[layer].reshape(1, -1).astype(F32), ((0, 0), (0, pad16)))
    a128 = jnp.pad(-jnp.exp(a_log[layer].astype(F32)).reshape(1, -1), ((0, 0), (0, pad16)))
    dskip512 = jnp.repeat(d_skip[layer].astype(F32), SSD_HEAD_DIM).reshape(1, SSD_W)
    conv_w8 = jnp.pad(conv_w[layer].astype(F32), ((0, 8 - SSD_CONV), (0, 0)))
    ssd = _ssd(xbc.reshape(b, seq_len, XBC_W), zs.reshape(b, seq_len, SSD_W), dtr,
               conv_w8, conv_b[layer].reshape(1, XBC_W).astype(F32), dt_bias128, a128, dskip512,
               ssd_norm_w[layer].reshape(1, SSD_W).astype(F32))

    r3 = lambda arr: arr.reshape(b, seq_len, arr.shape[-1])
    return _mix(x, r3(q), r3(k), r3(v), r3(ga), r3(qm), r3(gm), mk, mv, ssd,
                attn_sink[layer].reshape(1, ATT_HEADS).astype(F32), w_out[layer].astype(BF16), norm_out_w)
```

```python
import functools
import math

import jax
import jax.numpy as jnp
import numpy as np
from jax import lax
from jax.experimental import pallas as pl
from jax.experimental.pallas import tpu as pltpu

F32 = jnp.float32
BF16 = jnp.bfloat16

D_MODEL = 1024
ATT_HEADS = 16
ATT_KV_HEADS = 4
ATT_HEAD_DIM = 64
HALF = ATT_HEAD_DIM // 2
ATT_W = ATT_HEADS * ATT_HEAD_DIM
KV_W = ATT_KV_HEADS * ATT_HEAD_DIM
WINDOW = 128
BLOCK = 128
ROPE_THETA = 10000.0
SSD_HEADS = 8
SSD_HEAD_DIM = 64
SSD_W = SSD_HEADS * SSD_HEAD_DIM
SSD_GROUPS = 2
SSD_HPG = SSD_HEADS // SSD_GROUPS
SSD_STATE = 128
SSD_CONV = 5
CHUNK = 128
N_DIR = 2
XBC_W = SSD_W + 2 * SSD_GROUPS * SSD_STATE
MEM_LEN = 256
MEM_HEADS = 4
MEM_HEAD_DIM = 128
MEM_W = MEM_HEADS * MEM_HEAD_DIM
MIX_W = ATT_W + SSD_W + MEM_W
EPS = 1e-6

LANES = 128
GROUP_W = 256
HALO = 16
NEG = -1e30
VMEM_LIMIT = 56 * 1024 * 1024
LOG2E = math.log2(math.e)

PROJ_TM = 1024
PROJ_SUB = 256
MIX_TQ = 512
MIX_SUB = 256
SSD_CPS = 4


def _silu(v):
    return v * (1.0 / (1.0 + jnp.exp(-v)))


def _softplus(v):
    return jnp.maximum(v, 0.0) + jnp.log(1.0 + jnp.exp(-jnp.abs(v)))


def _dot(a, b):
    return jnp.dot(a, b, preferred_element_type=F32)


def _dot_nt(a, b):
    return lax.dot_general(a, b, (((1,), (1,)), ((), ())), preferred_element_type=F32)


def _split3(v):
    hi = v.astype(BF16)
    r1 = v - hi.astype(F32)
    mid = r1.astype(BF16)
    lo = (r1 - mid.astype(F32)).astype(BF16)
    return hi, mid, lo


def _memkv_kernel(mem_ref, nw_ref, w_ref, mk_ref, mv_ref):
    m = mem_ref[0]
    ms = jnp.mean(m * m, axis=-1, keepdims=True)
    mn = (m * lax.rsqrt(ms + EPS) * nw_ref[...]).astype(BF16)
    mk_ref[0] = _dot(mn, w_ref[:, :MEM_W]).astype(BF16)
    mv_ref[0] = _dot(mn, w_ref[:, MEM_W:]).astype(BF16)


def _memkv(mem, norm_mem_w, w_mem_kv_bf):
    b = mem.shape[0]
    return pl.pallas_call(
        _memkv_kernel,
        grid=(b,),
        in_specs=[
            pl.BlockSpec((1, MEM_LEN, D_MODEL), lambda i: (i, 0, 0)),
            pl.BlockSpec((1, D_MODEL), lambda i: (0, 0)),
            pl.BlockSpec((D_MODEL, 2 * MEM_W), lambda i: (0, 0)),
        ],
        out_specs=[
            pl.BlockSpec((1, MEM_LEN, MEM_W), lambda i: (i, 0, 0)),
            pl.BlockSpec((1, MEM_LEN, MEM_W), lambda i: (i, 0, 0)),
        ],
        out_shape=[jax.ShapeDtypeStruct((b, MEM_LEN, MEM_W), BF16)] * 2,
        compiler_params=pltpu.CompilerParams(
            dimension_semantics=("arbitrary",), vmem_limit_bytes=VMEM_LIMIT),
        name="memkv",
    )(mem, norm_mem_w.reshape(1, D_MODEL), w_mem_kv_bf)


def _proj_kernel(x_ref, nw_ref, cos_ref, sin_ref, *refs):
    n_w = 9
    w_refs, out_refs = refs[:n_w], refs[n_w:]
    for r0 in range(0, x_ref.shape[0], PROJ_SUB):
        rows = pl.ds(r0, PROJ_SUB)
        _proj_rows(x_ref.at[rows], nw_ref, cos_ref.at[rows], sin_ref.at[rows], *w_refs,
                   *[o.at[rows] for o in out_refs])


def _proj_rows(x_ref, nw_ref, cos_ref, sin_ref,
               wq_ref, wk_ref, wv_ref, wga_ref, wz_ref, wxbc_ref, wdt_ref, wqm_ref, wgm_ref,
               q_out, k_out, v_out, ga_out, z_out, xbc_out, dt_out, qm_out, gm_out):
    x = x_ref[...]
    ms = jnp.mean(x * x, axis=-1, keepdims=True)
    hn = (x * lax.rsqrt(ms + EPS) * nw_ref[...]).astype(BF16)
    c = cos_ref[...]
    s = sin_ref[...]
    att_scale = ATT_HEAD_DIM ** -0.5 * LOG2E
    mem_scale = MEM_HEAD_DIM ** -0.5 * LOG2E

    for g in range(ATT_KV_HEADS):
        t = _dot(hn, wq_ref[:, GROUP_W * g:GROUP_W * (g + 1)])
        t1 = t[:, :LANES]
        t2 = t[:, LANES:]
        q_out[:, GROUP_W * g:GROUP_W * g + LANES] = ((t1 * c - t2 * s) * att_scale).astype(BF16)
        q_out[:, GROUP_W * g + LANES:GROUP_W * (g + 1)] = ((t1 * s + t2 * c) * att_scale).astype(BF16)

    t = _dot(hn, wk_ref[...])
    t1 = t[:, :LANES]
    t2 = t[:, LANES:]
    ko = (t1 * c - t2 * s, t1 * s + t2 * c)
    slot32 = lax.broadcasted_iota(jnp.int32, t1.shape, 1) // HALF
    for half in range(2):
        r = [ko[half]] + [pltpu.roll(ko[half], HALF * m, 1) for m in range(1, 4)]
        for g in range(ATT_KV_HEADS):
            rep = r[(3 - g) % 4]
            for j in (2, 1, 0):
                rep = jnp.where(slot32 == j, r[(j - g) % 4], rep)
            k_out[:, GROUP_W * g + LANES * half:GROUP_W * g + LANES * (half + 1)] = rep.astype(BF16)

    t = _dot(hn, wv_ref[...])
    low = lax.broadcasted_iota(jnp.int32, (t.shape[0], LANES), 1) < ATT_HEAD_DIM
    for pair in range(2):
        a = t[:, LANES * pair:LANES * (pair + 1)]
        ra = pltpu.roll(a, ATT_HEAD_DIM, 1)
        even = jnp.where(low, a, ra).astype(BF16)
        odd = jnp.where(low, ra, a).astype(BF16)
        for rep in range(2):
            v_out[:, GROUP_W * (2 * pair) + LANES * rep:GROUP_W * (2 * pair) + LANES * (rep + 1)] = even
            v_out[:, GROUP_W * (2 * pair + 1) + LANES * rep:GROUP_W * (2 * pair + 1) + LANES * (rep + 1)] = odd

    for j in range(ATT_W // GROUP_W):
        t = _dot(hn, wga_ref[:, GROUP_W * j:GROUP_W * (j + 1)])
        ga_out[:, GROUP_W * j:GROUP_W * (j + 1)] = _silu(t).astype(BF16)
    for j in range(SSD_W // GROUP_W):
        t = _dot(hn, wz_ref[:, GROUP_W * j:GROUP_W * (j + 1)])
        z_out[:, GROUP_W * j:GROUP_W * (j + 1)] = _silu(t).astype(BF16)
    for j in range(XBC_W // GROUP_W):
        t = _dot(hn, wxbc_ref[:, GROUP_W * j:GROUP_W * (j + 1)])
        xbc_out[:, GROUP_W * j:GROUP_W * (j + 1)] = t.astype(BF16)
    dt_out[...] = _dot(hn, wdt_ref[...])
    for j in range(MEM_W // GROUP_W):
        t = _dot(hn, wqm_ref[:, GROUP_W * j:GROUP_W * (j + 1)])
        qm_out[:, GROUP_W * j:GROUP_W * (j + 1)] = (t * mem_scale).astype(BF16)
        t = _dot(hn, wgm_ref[:, GROUP_W * j:GROUP_W * (j + 1)])
        gm_out[:, GROUP_W * j:GROUP_W * (j + 1)] = _silu(t).astype(BF16)


def _proj(x2, norm_w, cos128, sin128, weights, seq_len):
    t_rows = x2.shape[0]
    tm = PROJ_TM
    tiles_per_seq = seq_len // tm
    widths = (ATT_W, ATT_W, ATT_W, ATT_W, SSD_W, XBC_W, LANES, MEM_W, MEM_W)
    dtypes = (BF16, BF16, BF16, BF16, BF16, BF16, F32, BF16, BF16)
    row = lambda i: (i, 0)
    const = lambda i: (0, 0)
    in_specs = [
        pl.BlockSpec((tm, D_MODEL), row),
        pl.BlockSpec((1, D_MODEL), const),
        pl.BlockSpec((tm, LANES), lambda i: (i % tiles_per_seq, 0)),
        pl.BlockSpec((tm, LANES), lambda i: (i % tiles_per_seq, 0)),
    ] + [pl.BlockSpec(w.shape, const, pipeline_mode=pl.Buffered(1)) for w in weights]
    return pl.pallas_call(
        _proj_kernel,
        grid=(t_rows // tm,),
        in_specs=in_specs,
        out_specs=[pl.BlockSpec((tm, w), row) for w in widths],
        out_shape=[jax.ShapeDtypeStruct((t_rows, w), d) for w, d in zip(widths, dtypes)],
        compiler_params=pltpu.CompilerParams(
            dimension_semantics=("arbitrary",), vmem_limit_bytes=VMEM_LIMIT),
        name="proj",
    )(x2, norm_w.reshape(1, D_MODEL), cos128, sin128, *weights)


TL_ROWS = HALO


def _ssd_expand(mats, tl_part, e):
    lhs = jnp.concatenate(list(mats) + list(_split3(tl_part)), axis=0)
    r = _dot(lhs, e)
    n = len(mats) * CHUNK
    outs = [r[CHUNK * m:CHUNK * (m + 1), :] for m in range(len(mats))]
    dec = jnp.exp2(r[n:n + TL_ROWS, :] + r[n + TL_ROWS:n + 2 * TL_ROWS, :] + r[n + 2 * TL_ROWS:n + 3 * TL_ROWS, :])
    return outs, dec


def _ssd_state_update(st_ref, xc, wexp, dec):
    for g in range(SSD_GROUPS):
        lo = GROUP_W * g
        xs = xc[:, lo:lo + GROUP_W]
        bm = xc[:, SSD_W + SSD_STATE * g:SSD_W + SSD_STATE * (g + 1)]
        xw = (xs * wexp[:, lo:lo + GROUP_W]).astype(BF16)
        upd = _dot(bm.T.astype(BF16), xw)
        st_ref[g] = st_ref[g] * dec[0:1, lo:lo + GROUP_W] + upd


def _ssdl_kernel(xl_ref, xm_ref, xr_ref, dt_ref, cw_ref, cb_ref, dtb_ref, a_ref, shift_ref, eb_ref,
                 xc_out, zl_out, rt_out, rs_out, sw_out, tl_out, gb_out,
                 ext_ref, gb_ref, *, n_steps):
    t = pl.program_id(1)
    step = n_steps - 1 - t
    cps = xm_ref.shape[1] // CHUNK

    def decay_tables(k):
        rr = lax.broadcasted_iota(jnp.int32, (CHUNK, CHUNK), 0)
        cc = lax.broadcasted_iota(jnp.int32, (CHUNK, CHUNK), 1)
        fwd_col = cc < SSD_HEADS
        dt = _softplus(dt_ref[CHUNK * k:CHUNK * (k + 1), :] + dtb_ref[...])
        adt = dt * (a_ref[...] * LOG2E)
        cum = jnp.where(rr >= cc, 1.0, 0.0).astype(BF16)
        cum = jnp.concatenate([cum, jnp.ones((CHUNK, CHUNK), BF16)], axis=0)
        r3 = _dot(cum, jnp.concatenate(_split3(adt), axis=1))
        r3 = r3[:, 0:LANES] + r3[:, LANES:2 * LANES] + r3[:, 2 * LANES:]
        cs = r3[0:CHUNK, :]
        tl = r3[CHUNK:, :]
        zl = jnp.where(fwd_col, cs, cs - adt)
        ldt = jnp.log2(dt)
        rt = jnp.where(fwd_col, zl - ldt, zl + ldt).T
        rs = jnp.exp2(jnp.where(fwd_col, zl, tl - zl)).astype(BF16)
        sw = (jnp.exp2(jnp.where(fwd_col, tl - zl, zl)) * dt).astype(BF16)
        zl_out[CHUNK * k:CHUNK * (k + 1), :] = zl
        rt_out[k] = rt[0:N_DIR * SSD_HEADS, :]
        rs_out[CHUNK * k:CHUNK * (k + 1), :] = rs
        sw_out[CHUNK * k:CHUNK * (k + 1), :] = sw
        tl_out[k] = tl[0:TL_ROWS, :]
        return sw, tl[0:TL_ROWS, :]

    def conv_silu(k):
        ext = ext_ref[CHUNK * k:CHUNK * (k + 2), :]
        pad = (SSD_CONV - 1) // 2
        acc = cb_ref[...] + cw_ref[pad:pad + 1, :] * xm_ref[0, CHUNK * k:CHUNK * (k + 1), :].astype(F32)
        shifted = _dot(shift_ref[...], ext)
        tap = 0
        for j in range(SSD_CONV):
            if j == pad:
                continue
            acc = acc + cw_ref[j:j + 1, :] * shifted[CHUNK * tap:CHUNK * (tap + 1), :]
            tap += 1
        xc = _silu(acc)
        xc_out[0, CHUNK * k:CHUNK * (k + 1), :] = xc.astype(BF16)
        return xc

    @pl.when(t == 0)
    def _():
        gb_ref[...] = jnp.zeros_like(gb_ref)
        tail = CHUNK * cps + 2 * HALO
        ext_ref[tail:, :] = jnp.zeros((ext_ref.shape[0] - tail, XBC_W), BF16)

    lflag = jnp.where(step == 0, 0.0, 1.0)
    rflag = jnp.where(step == n_steps - 1, 0.0, 1.0)
    ext_ref[0:HALO, :] = (xl_ref[0].astype(F32) * lflag).astype(BF16)
    ext_ref[HALO:HALO + CHUNK * cps, :] = xm_ref[0]
    ext_ref[HALO + CHUNK * cps:2 * HALO + CHUNK * cps, :] = (xr_ref[0].astype(F32) * rflag).astype(BF16)
    order = list(reversed(range(cps)))
    tables = {k: decay_tables(k) for k in order}
    xcs = {k: conv_silu(k) for k in order}
    for k in order:
        sw, tl_part = tables[k]
        gb_out[k] = gb_ref[...].astype(BF16)
        (wexp,), dec = _ssd_expand([sw], tl_part, eb_ref[...])
        _ssd_state_update(gb_ref, xcs[k], wexp, dec)


def _ssd_right(xc_bf, zl, rt, rs, sw, tl_part, gb_next, zgate, dsk_ref, nw_ref, ef_ref, eb_ref, hf_ref, y_ref, rows):
    rr = lax.broadcasted_iota(jnp.int32, (CHUNK, CHUNK), 0)
    cc = lax.broadcasted_iota(jnp.int32, (CHUNK, CHUNK), 1)
    lower = rr >= cc
    upper = rr <= cc
    slot = lax.broadcasted_iota(jnp.int32, (CHUNK, GROUP_W), 1) // SSD_HEAD_DIM
    slotmask = [jnp.where(slot == j, 1.0, 0.0).astype(BF16) for j in range(SSD_HPG)]

    xc = xc_bf.astype(F32)
    scale_f = _dot(rs, ef_ref[...])
    scale_b = _dot(rs, eb_ref[...])
    for g in range(SSD_GROUPS):
        lo = GROUP_W * g
        xs = xc[:, lo:lo + GROUP_W]
        bm = xc_bf[:, SSD_W + SSD_STATE * g:SSD_W + SSD_STATE * (g + 1)]
        cm = xc_bf[:, SSD_W + SSD_GROUPS * SSD_STATE + SSD_STATE * g:
                   SSD_W + SSD_GROUPS * SSD_STATE + SSD_STATE * (g + 1)]
        gmat = _dot_nt(cm, bm)
        xs_bf = xc_bf[:, lo:lo + GROUP_W]
        m_parts = []
        x_parts = []
        for j in range(SSD_HPG):
            hf = SSD_HPG * g + j
            hb = SSD_HEADS + hf
            w_f = jnp.where(lower, jnp.exp2(zl[:, hf:hf + 1] - rt[hf:hf + 1, :]), 0.0)
            w_b = jnp.where(upper, jnp.exp2(rt[hb:hb + 1, :] - zl[:, hb:hb + 1]), 0.0)
            m_parts.append((gmat * (w_f + w_b)).astype(BF16))
            x_parts.append(xs_bf * slotmask[j])
        m_cat = jnp.concatenate(m_parts, axis=1)
        x_stack = jnp.concatenate(x_parts, axis=0)
        y = _dot(m_cat, x_stack)
        y = y + _dot(cm, hf_ref[g].astype(BF16)) * scale_f[:, lo:lo + GROUP_W]
        y = y + _dot(cm, gb_next[g]) * scale_b[:, lo:lo + GROUP_W]
        y = y + dsk_ref[:, lo:lo + GROUP_W] * xs
        gated = y * zgate[:, lo:lo + GROUP_W].astype(F32)
        ms = jnp.mean(gated * gated, axis=-1, keepdims=True)
        y_ref[rows, lo:lo + GROUP_W] = (gated * lax.rsqrt(ms + EPS) * nw_ref[:, lo:lo + GROUP_W]).astype(BF16)
    (wexp,), dec = _ssd_expand([sw], tl_part, ef_ref[...])
    _ssd_state_update(hf_ref, xc, wexp, dec)


def _ssd_tables():
    pad = (SSD_CONV - 1) // 2
    win_rows = 2 * CHUNK
    rows = np.arange(CHUNK)[:, None]
    cols = np.arange(win_rows)[None, :]
    shift = np.concatenate([(cols == rows + HALO - pad + j) for j in range(SSD_CONV) if j != pad]).astype(np.float32)
    hcol = np.arange(LANES)[:, None]
    hslot = (np.arange(SSD_W) // SSD_HEAD_DIM)[None, :]
    e_f = (hcol == hslot).astype(np.float32)
    e_b = (hcol == hslot + SSD_HEADS).astype(np.float32)
    return jnp.asarray(shift, BF16), jnp.asarray(e_f, BF16), jnp.asarray(e_b, BF16)


def _ssdl(xbc, dtr, conv_w8, conv_b, dt_bias128, a128, shift, e_b):
    b, seq_len, _ = xbc.shape
    n_chunks = seq_len // CHUNK
    cps = SSD_CPS
    rows_per_step = CHUNK * cps
    n_steps = n_chunks // cps
    halo_per_step = rows_per_step // HALO
    n_halo = seq_len // HALO
    ext_rows = CHUNK * (cps + 1)
    t_rows = b * seq_len

    step_of = lambda t: n_steps - 1 - t
    const2 = lambda i, t: (0, 0)
    in_specs = [
        pl.BlockSpec((1, HALO, XBC_W), lambda i, t: (i, jnp.maximum(step_of(t) * halo_per_step - 1, 0), 0)),
        pl.BlockSpec((1, rows_per_step, XBC_W), lambda i, t: (i, step_of(t), 0)),
        pl.BlockSpec((1, HALO, XBC_W),
                     lambda i, t: (i, jnp.minimum((step_of(t) + 1) * halo_per_step, n_halo - 1), 0)),
        pl.BlockSpec((rows_per_step, LANES), lambda i, t: (i * n_steps + step_of(t), 0)),
        pl.BlockSpec((8, XBC_W), const2),
        pl.BlockSpec((1, XBC_W), const2),
        pl.BlockSpec((1, LANES), const2),
        pl.BlockSpec((1, LANES), const2),
        pl.BlockSpec(shift.shape, const2),
        pl.BlockSpec((LANES, SSD_W), const2),
    ]
    rows_map = lambda i, t: (i * n_steps + step_of(t), 0)
    chunk_map3 = lambda i, t: (i * n_steps + step_of(t), 0, 0)
    chunk_map4 = lambda i, t: (i * n_steps + step_of(t), 0, 0, 0)
    out_specs = [
        pl.BlockSpec((1, rows_per_step, XBC_W), lambda i, t: (i, step_of(t), 0)),
        pl.BlockSpec((rows_per_step, LANES), rows_map),
        pl.BlockSpec((cps, N_DIR * SSD_HEADS, CHUNK), chunk_map3),
        pl.BlockSpec((rows_per_step, LANES), rows_map),
        pl.BlockSpec((rows_per_step, LANES), rows_map),
        pl.BlockSpec((cps, TL_ROWS, LANES), chunk_map3),
        pl.BlockSpec((cps, SSD_GROUPS, SSD_STATE, GROUP_W), chunk_map4),
    ]
    out_shape = [
        jax.ShapeDtypeStruct((b, seq_len, XBC_W), BF16),
        jax.ShapeDtypeStruct((t_rows, LANES), F32),
        jax.ShapeDtypeStruct((b * n_chunks, N_DIR * SSD_HEADS, CHUNK), F32),
        jax.ShapeDtypeStruct((t_rows, LANES), BF16),
        jax.ShapeDtypeStruct((t_rows, LANES), BF16),
        jax.ShapeDtypeStruct((b * n_chunks, TL_ROWS, LANES), F32),
        jax.ShapeDtypeStruct((b * n_chunks, SSD_GROUPS, SSD_STATE, GROUP_W), BF16),
    ]
    return pl.pallas_call(
        functools.partial(_ssdl_kernel, n_steps=n_steps),
        grid=(b, n_steps),
        in_specs=in_specs,
        out_specs=out_specs,
        out_shape=out_shape,
        scratch_shapes=[
            pltpu.VMEM((ext_rows, XBC_W), BF16),
            pltpu.VMEM((SSD_GROUPS, SSD_STATE, GROUP_W), F32),
        ],
        compiler_params=pltpu.CompilerParams(
            dimension_semantics=("arbitrary", "arbitrary"), vmem_limit_bytes=VMEM_LIMIT),
        name="ssdl",
    )(xbc, xbc, xbc, dtr, conv_w8, conv_b, dt_bias128, a128, shift, e_b)


def _mix_kernel(*refs, n_tiles):
    p_ref = refs[-5]
    nblk = MIX_SUB // BLOCK
    pstride = 4 * BLOCK

    @pl.when(jnp.logical_and(pl.program_id(0) == 0, pl.program_id(1) == 0))
    def _():
        for blk in range(nblk):
            for p in range(nblk + 2):
                if p < blk or p > blk + 2:
                    p_ref[:, blk * BLOCK:(blk + 1) * BLOCK, pstride * p:pstride * (p + 1)] = jnp.zeros(
                        (ATT_KV_HEADS, BLOCK, pstride), BF16)

    for sub in range(refs[1].shape[1] // MIX_SUB):
        _mix_rows(sub, *refs, n_tiles=n_tiles)


def _mix_rows(sub, x_ref, q_ref, kl_ref, km_ref, kr_ref, vl_ref, vm_ref, vr_ref, ga_ref,
              qm_ref, gm_ref, mk_ref, mv_ref,
              xc_ref, zl_ref, rt_ref, rs_ref, sw_ref, tl_ref, gb_ref, zg_ref, dsk_ref, snw_ref, ef_ref, eb_ref,
              sink_ref, wo_ref, now_ref,
              o_ref, kst_ref, vst_ref, p_ref, att_ref, xat_ref, ssd_ref, hf_ref, *, n_tiles):
    i = pl.program_id(1)
    tq = MIX_SUB
    nsub = q_ref.shape[1] // tq
    nblk = tq // BLOCK
    rows = pl.ds(sub * tq, tq)

    if sub == 0:
        @pl.when(i == 0)
        def _():
            hf_ref[...] = jnp.zeros_like(hf_ref)
    for kk in range(tq // CHUNK):
        ck = sub * (tq // CHUNK) + kk
        crow = pl.ds(sub * tq + kk * CHUNK, CHUNK)
        _ssd_right(xc_ref[0, crow, :], zl_ref[crow, :], rt_ref[ck], rs_ref[crow, :], sw_ref[crow, :], tl_ref[ck],
                   gb_ref.at[ck], zg_ref[0, crow, :], dsk_ref, snw_ref, ef_ref, eb_ref, hf_ref, ssd_ref, crow)

    rr = lax.broadcasted_iota(jnp.int32, (BLOCK, WINDOW), 0)
    cc = lax.broadcasted_iota(jnp.int32, (BLOCK, WINDOW), 1)
    left_bias0 = jnp.where(cc >= rr, 0.0, NEG)
    right_bias0 = jnp.where(cc <= rr, 0.0, NEG)
    oslot = lax.broadcasted_iota(jnp.int32, (BLOCK, GROUP_W), 1) // ATT_HEAD_DIM
    pstride = 4 * BLOCK

    slot = lax.broadcasted_iota(jnp.int32, (BLOCK, GROUP_W), 1)
    kmask = [jnp.where(slot % LANES // HALF == j, 1.0, 0.0).astype(BF16) for j in range(4)]
    vmask = [jnp.where(slot // ATT_HEAD_DIM == j, 1.0, 0.0).astype(BF16) for j in range(4)]

    def piece(l_ref, m_ref, r_ref, p, lo):
        gp = sub * nblk + p
        if gp == 0:
            return l_ref[0, :, lo:lo + GROUP_W]
        if gp == nsub * nblk + 1:
            return r_ref[0, :, lo:lo + GROUP_W]
        return m_ref[0, (gp - 1) * BLOCK:gp * BLOCK, lo:lo + GROUP_W]

    def build_stacks(g):
        lo = GROUP_W * g
        for p in range(nblk + 2):
            kp = piece(kl_ref, km_ref, kr_ref, p, lo)
            vp = piece(vl_ref, vm_ref, vr_ref, p, lo)
            for j in range(4):
                rows = pl.ds(pstride * p + BLOCK * j, BLOCK)
                kst_ref[g, rows, :] = kp * kmask[j]
                vst_ref[g, rows, :] = vp * vmask[j]

    def scores(g):
        return _dot_nt(q_ref[0, rows, GROUP_W * g:GROUP_W * (g + 1)], kst_ref[g])

    def finish(g, s_all):
        lo = GROUP_W * g
        inv_rows = []
        for blk in range(nblk):
            r0 = blk * BLOCK
            left_dead = jnp.logical_and(i == 0, sub == 0 and blk == 0)
            right_dead = jnp.logical_and(i == n_tiles - 1, sub == nsub - 1 and blk == nblk - 1)
            left_bias = left_bias0 + jnp.where(left_dead, NEG, 0.0)
            right_bias = right_bias0 + jnp.where(right_dead, NEG, 0.0)
            inv = jnp.zeros((BLOCK, GROUP_W), F32)
            for j in range(4):
                c_l = pstride * blk + BLOCK * j
                c_m = c_l + pstride
                c_r = c_m + pstride
                s_l = s_all[r0:r0 + BLOCK, c_l:c_l + BLOCK] + left_bias
                s_m = s_all[r0:r0 + BLOCK, c_m:c_m + BLOCK]
                s_r = s_all[r0:r0 + BLOCK, c_r:c_r + BLOCK] + right_bias
                sink = sink_ref[0, 4 * g + j] * LOG2E
                m = jnp.maximum(jnp.max(jnp.maximum(jnp.maximum(s_l, s_m), s_r), axis=-1, keepdims=True), sink)
                p_l = jnp.exp2(s_l - m)
                p_m = jnp.exp2(s_m - m)
                p_r = jnp.exp2(s_r - m)
                den = jnp.sum(p_l + p_m + p_r, axis=-1, keepdims=True) + jnp.exp2(sink - m)
                p_ref[g, r0:r0 + BLOCK, c_l:c_l + BLOCK] = p_l.astype(BF16)
                p_ref[g, r0:r0 + BLOCK, c_m:c_m + BLOCK] = p_m.astype(BF16)
                p_ref[g, r0:r0 + BLOCK, c_r:c_r + BLOCK] = p_r.astype(BF16)
                inv = jnp.where(oslot == j, jnp.broadcast_to(1.0 / den, inv.shape), inv)
            inv_rows.append(inv)
        o = _dot(p_ref[g], vst_ref[g]) * jnp.concatenate(inv_rows, axis=0)
        gate = ga_ref[0, rows, lo:lo + GROUP_W].astype(F32)
        att_ref[rows, lo:lo + GROUP_W] = (o * gate).astype(BF16)

    build_stacks(0)
    s_next = scores(0)
    for g in range(ATT_KV_HEADS):
        s_cur = s_next
        if g + 1 < ATT_KV_HEADS:
            build_stacks(g + 1)
            s_next = scores(g + 1)
        finish(g, s_cur)

    for h in range(MEM_HEADS):
        lo = MEM_HEAD_DIM * h
        s = _dot_nt(qm_ref[0, rows, lo:lo + MEM_HEAD_DIM], mk_ref[0, :, lo:lo + MEM_HEAD_DIM])
        m = jnp.max(s, axis=-1, keepdims=True)
        p = jnp.exp2(s - m)
        den = jnp.sum(p, axis=-1, keepdims=True)
        o = _dot(p.astype(BF16), mv_ref[0, :, lo:lo + MEM_HEAD_DIM]) * (1.0 / den)
        xat_ref[rows, lo:lo + MEM_HEAD_DIM] = (o * gm_ref[0, rows, lo:lo + MEM_HEAD_DIM].astype(F32)).astype(BF16)

    delta = _dot(att_ref[rows, :], wo_ref[0:ATT_W, :])
    delta = delta + _dot(ssd_ref[rows, :], wo_ref[ATT_W:ATT_W + SSD_W, :])
    delta = delta + _dot(xat_ref[rows, :], wo_ref[ATT_W + SSD_W:, :])
    hres = x_ref[0, rows, :] + delta
    ms = jnp.mean(hres * hres, axis=-1, keepdims=True)
    o_ref[0, rows, :] = hres * lax.rsqrt(ms + EPS) * now_ref[...]


def _mix(x, q, k, v, ga, qm, gm, mk, mv, xc, zl, rt, rs, sw, tl, gb, zs, dskip, ssd_norm_w, e_f, e_b,
         sink, w_out_bf, norm_out_w):
    b, seq_len, _ = x.shape
    tq = MIX_TQ
    n_tiles = seq_len // tq
    bpt = tq // BLOCK
    sub_blocks = MIX_SUB // BLOCK
    n_blocks = seq_len // BLOCK
    span = BLOCK + 2 * WINDOW
    main = lambda w: pl.BlockSpec((1, tq, w), lambda bi, i: (bi, i, 0))
    left = lambda w: pl.BlockSpec((1, BLOCK, w), lambda bi, i: (bi, jnp.maximum(i * bpt - 1, 0), 0))
    right = lambda w: pl.BlockSpec((1, BLOCK, w), lambda bi, i: (bi, jnp.minimum((i + 1) * bpt, n_blocks - 1), 0))
    per_b = lambda r, w: pl.BlockSpec((1, r, w), lambda bi, i: (bi, 0, 0))
    const2 = lambda bi, i: (0, 0)
    cpt = tq // CHUNK
    tile_rows = pl.BlockSpec((tq, LANES), lambda bi, i: (bi * n_tiles + i, 0))
    per_chunk = lambda r, w: pl.BlockSpec((cpt, r, w), lambda bi, i: (bi * n_tiles + i, 0, 0))
    in_specs = [
        main(D_MODEL), main(ATT_W),
        left(ATT_W), main(ATT_W), right(ATT_W),
        left(ATT_W), main(ATT_W), right(ATT_W),
        main(ATT_W), main(MEM_W), main(MEM_W),
        per_b(MEM_LEN, MEM_W), per_b(MEM_LEN, MEM_W),
        main(XBC_W), tile_rows, per_chunk(N_DIR * SSD_HEADS, CHUNK), tile_rows, tile_rows,
        per_chunk(TL_ROWS, LANES),
        pl.BlockSpec((cpt, SSD_GROUPS, SSD_STATE, GROUP_W), lambda bi, i: (bi * n_tiles + i, 0, 0, 0)),
        main(SSD_W),
        pl.BlockSpec((1, SSD_W), const2), pl.BlockSpec((1, SSD_W), const2),
        pl.BlockSpec((LANES, SSD_W), const2), pl.BlockSpec((LANES, SSD_W), const2),
        pl.BlockSpec(memory_space=pltpu.SMEM),
        pl.BlockSpec((MIX_W, D_MODEL), const2),
        pl.BlockSpec((1, D_MODEL), const2),
    ]
    return pl.pallas_call(
        functools.partial(_mix_kernel, n_tiles=n_tiles),
        grid=(b, n_tiles),
        in_specs=in_specs,
        out_specs=pl.BlockSpec((1, tq, D_MODEL), lambda bi, i: (bi, i, 0)),
        out_shape=jax.ShapeDtypeStruct((b, seq_len, D_MODEL), F32),
        scratch_shapes=[
            pltpu.VMEM((ATT_KV_HEADS, (sub_blocks + 2) * 4 * BLOCK, GROUP_W), BF16),
            pltpu.VMEM((ATT_KV_HEADS, (sub_blocks + 2) * 4 * BLOCK, GROUP_W), BF16),
            pltpu.VMEM((ATT_KV_HEADS, MIX_SUB, (sub_blocks + 2) * 4 * BLOCK), BF16),
            pltpu.VMEM((tq, ATT_W), BF16),
            pltpu.VMEM((tq, MEM_W), BF16),
            pltpu.VMEM((tq, SSD_W), BF16),
            pltpu.VMEM((SSD_GROUPS, SSD_STATE, GROUP_W), F32),
        ],
        compiler_params=pltpu.CompilerParams(
            dimension_semantics=("arbitrary", "arbitrary"), vmem_limit_bytes=VMEM_LIMIT),
        name="mix",
    )(x, q, k, k, k, v, v, v, ga, qm, gm, mk, mv, xc, zl, rt, rs, sw, tl, gb, zs, dskip, ssd_norm_w, e_f, e_b,
      sink, w_out_bf, norm_out_w.reshape(1, D_MODEL))


def _q_perm():
    idx = np.zeros((ATT_W,), np.int32)
    for g in range(ATT_KV_HEADS):
        for half in range(2):
            for j in range(4):
                for d in range(HALF):
                    idx[GROUP_W * g + LANES * half + HALF * j + d] = (4 * g + j) * ATT_HEAD_DIM + HALF * half + d
    return idx


def _k_perm():
    idx = np.zeros((KV_W,), np.int32)
    for half in range(2):
        for g in range(ATT_KV_HEADS):
            for d in range(HALF):
                idx[LANES * half + HALF * g + d] = g * ATT_HEAD_DIM + HALF * half + d
    return idx


def kernel(x, mem, norm_mem_w, norm_in_w, w_in, attn_sink, conv_w, conv_b, dt_bias, a_log, d_skip,
           ssd_norm_w, w_mem_kv, w_out, norm_out_w):
    b, seq_len, _ = x.shape
    depth = w_in.shape[0]

    pos = jnp.arange(seq_len, dtype=F32)
    inv_freq = 1.0 / (ROPE_THETA ** (jnp.arange(0, ATT_HEAD_DIM, 2, dtype=F32) / ATT_HEAD_DIM))
    ang = pos[:, None] * inv_freq[None, :]
    cos128 = jnp.tile(jnp.cos(ang), (1, 4))
    sin128 = jnp.tile(jnp.sin(ang), (1, 4))

    assert depth == 1, "the final norm is fused into the single layer's output stage"
    layer = 0
    w = w_in[layer]
    o = 0
    parts = []
    for width in (ATT_W, KV_W, KV_W, ATT_W, SSD_W, XBC_W, N_DIR * SSD_HEADS, MEM_W, MEM_W):
        parts.append(w[:, o:o + width])
        o += width
    wq, wk, wv, wga, wz, wxbc, wdt, wqm, wgm = parts
    wq = wq[:, _q_perm()]
    wk = wk[:, _k_perm()]
    pad16 = LANES - N_DIR * SSD_HEADS
    wdt = jnp.pad(wdt, ((0, 0), (0, pad16)))
    weights = [m.astype(BF16) for m in (wq, wk, wv, wga, wz, wxbc, wdt, wqm, wgm)]

    mk, mv = _memkv(mem, norm_mem_w, w_mem_kv[layer].astype(BF16))
    q, k, v, ga, zs, xbc, dtr, qm, gm = _proj(
        x.reshape(b * seq_len, D_MODEL), norm_in_w[layer], cos128, sin128, weights, seq_len)

    dt_bias128 = jnp.pad(dt_bias[layer].reshape(1, -1).astype(F32), ((0, 0), (0, pad16)))
    a128 = jnp.pad(-jnp.exp(a_log[layer].astype(F32)).reshape(1, -1), ((0, 0), (0, pad16)))
    dskip512 = jnp.repeat(d_skip[layer].astype(F32), SSD_HEAD_DIM).reshape(1, SSD_W)
    conv_w8 = jnp.pad(conv_w[layer].astype(F32), ((0, 8 - SSD_CONV), (0, 0)))
    shift, e_f, e_b = _ssd_tables()
    xc, zl, rt, rs, sw, tl, gb = _ssdl(xbc.reshape(b, seq_len, XBC_W), dtr, conv_w8,
                                       conv_b[layer].reshape(1, XBC_W).astype(F32), dt_bias128, a128, shift, e_b)

    r3 = lambda arr: arr.reshape(b, seq_len, arr.shape[-1])
    return _mix(x, r3(q), r3(k), r3(v), r3(ga), r3(qm), r3(gm), mk, mv, xc, zl, rt, rs, sw, tl, gb, r3(zs),
                dskip512, ssd_norm_w[layer].reshape(1, SSD_W).astype(F32), e_f, e_b,
                attn_sink[layer].reshape(1, ATT_HEADS).astype(F32), w_out[layer].astype(BF16), norm_out_w)
```

```python
import functools
import math

import jax
import jax.numpy as jnp
import numpy as np
from jax import lax
from jax.experimental import pallas as pl
from jax.experimental.pallas import tpu as pltpu

F32 = jnp.float32
BF16 = jnp.bfloat16

D_MODEL = 1024
ATT_HEADS = 16
ATT_KV_HEADS = 4
ATT_HEAD_DIM = 64
HALF = ATT_HEAD_DIM // 2
ATT_W = ATT_HEADS * ATT_HEAD_DIM
KV_W = ATT_KV_HEADS * ATT_HEAD_DIM
WINDOW = 128
BLOCK = 128
ROPE_THETA = 10000.0
SSD_HEADS = 8
SSD_HEAD_DIM = 64
SSD_W = SSD_HEADS * SSD_HEAD_DIM
SSD_GROUPS = 2
SSD_HPG = SSD_HEADS // SSD_GROUPS
SSD_STATE = 128
SSD_CONV = 5
CHUNK = 128
N_DIR = 2
XBC_W = SSD_W + 2 * SSD_GROUPS * SSD_STATE
MEM_LEN = 256
MEM_HEADS = 4
MEM_HEAD_DIM = 128
MEM_W = MEM_HEADS * MEM_HEAD_DIM
MIX_W = ATT_W + SSD_W + MEM_W
EPS = 1e-6

LANES = 128
GROUP_W = 256
HALO = 16
NEG = -1e30
VMEM_LIMIT = 56 * 1024 * 1024
LOG2E = math.log2(math.e)

PROJ_TM = 512
PROJ_SUB = 256
MIX_TQ = 512
MIX_SUB = 256
SSD_CPS = 4
UNITS_PER_SUB = 16


def _silu(v):
    return v * (1.0 / (1.0 + jnp.exp(-v)))


def _softplus(v):
    return jnp.maximum(v, 0.0) + jnp.log(1.0 + jnp.exp(-jnp.abs(v)))


def _dot(a, b):
    return jnp.dot(a, b, preferred_element_type=F32)


def _dot_nt(a, b):
    return lax.dot_general(a, b, (((1,), (1,)), ((), ())), preferred_element_type=F32)


def _split3(v):
    hi = v.astype(BF16)
    r1 = v - hi.astype(F32)
    mid = r1.astype(BF16)
    lo = (r1 - mid.astype(F32)).astype(BF16)
    return hi, mid, lo


def _memkv_kernel(mem_ref, nw_ref, w_ref, mk_ref, mv_ref):
    m = mem_ref[0]
    ms = jnp.mean(m * m, axis=-1, keepdims=True)
    mn = (m * lax.rsqrt(ms + EPS) * nw_ref[...]).astype(BF16)
    mk_ref[0] = _dot(mn, w_ref[:, :MEM_W]).astype(BF16)
    mv_ref[0] = _dot(mn, w_ref[:, MEM_W:]).astype(BF16)


def _memkv(mem, norm_mem_w, w_mem_kv_bf):
    b = mem.shape[0]
    return pl.pallas_call(
        _memkv_kernel,
        grid=(b,),
        in_specs=[
            pl.BlockSpec((1, MEM_LEN, D_MODEL), lambda i: (i, 0, 0)),
            pl.BlockSpec((1, D_MODEL), lambda i: (0, 0)),
            pl.BlockSpec((D_MODEL, 2 * MEM_W), lambda i: (0, 0)),
        ],
        out_specs=[
            pl.BlockSpec((1, MEM_LEN, MEM_W), lambda i: (i, 0, 0)),
            pl.BlockSpec((1, MEM_LEN, MEM_W), lambda i: (i, 0, 0)),
        ],
        out_shape=[jax.ShapeDtypeStruct((b, MEM_LEN, MEM_W), BF16)] * 2,
        compiler_params=pltpu.CompilerParams(
            dimension_semantics=("arbitrary",), vmem_limit_bytes=VMEM_LIMIT),
        name="memkv",
    )(mem, norm_mem_w.reshape(1, D_MODEL), w_mem_kv_bf)


TL_ROWS = HALO
N_PROJ_W = 9


def _ssd_expand(mats, tl_part, e):
    lhs = jnp.concatenate(list(mats) + list(_split3(tl_part)), axis=0)
    r = _dot(lhs, e)
    n = len(mats) * CHUNK
    outs = [r[CHUNK * m:CHUNK * (m + 1), :] for m in range(len(mats))]
    dec = jnp.exp2(r[n:n + TL_ROWS, :] + r[n + TL_ROWS:n + 2 * TL_ROWS, :] + r[n + 2 * TL_ROWS:n + 3 * TL_ROWS, :])
    return outs, dec


def _ssd_state_update(st_ref, xc, wexp, dec):
    for g in range(SSD_GROUPS):
        lo = GROUP_W * g
        xs = xc[:, lo:lo + GROUP_W]
        bm = xc[:, SSD_W + SSD_STATE * g:SSD_W + SSD_STATE * (g + 1)]
        xw = (xs * wexp[:, lo:lo + GROUP_W]).astype(BF16)
        upd = _dot(bm.T.astype(BF16), xw)
        st_ref[g] = st_ref[g] * dec[0:1, lo:lo + GROUP_W] + upd


def _proj_kernel(x_ref, xh_ref, nw_ref, cos_ref, sin_ref, *refs, tiles_per_seq):
    w_refs = refs[:N_PROJ_W]
    cw_ref, cb_ref, dtb_ref, a_ref, shift_ref, eb_ref = refs[N_PROJ_W:N_PROJ_W + 6]
    outs = refs[N_PROJ_W + 6:]
    q_out, k_out, v_out, ga_out, z_out, qm_out, gm_out = outs[:7]
    xc_out, zl_out, rt_out, rs_out, sw_out, tl_out, gb_out = outs[7:14]
    win_ref, dt_ref, gb_ref = outs[14:]

    tm = x_ref.shape[0]
    cps = tm // CHUNK
    j = tiles_per_seq - 1 - pl.program_id(0) % tiles_per_seq
    first_step = j == tiles_per_seq - 1
    main0 = HALO

    @pl.when(pl.program_id(0) == 0)
    def _():
        tail = tm + 2 * HALO
        win_ref[tail:, :] = jnp.zeros((win_ref.shape[0] - tail, XBC_W), BF16)

    @pl.when(first_step)
    def _():
        gb_ref[...] = jnp.zeros_like(gb_ref)

    prev_head = win_ref[main0:main0 + HALO, :]
    win_ref[main0 + tm:main0 + tm + HALO, :] = jnp.where(first_step, jnp.zeros_like(prev_head), prev_head)

    subs = list(reversed(range(0, tm, PROJ_SUB)))
    hns = {}
    for r0 in subs:
        rows = pl.ds(r0, PROJ_SUB)
        halo = (xh_ref, j == 0) if r0 == 0 else None
        hns[r0] = _proj_conv_cols(x_ref.at[rows], halo, nw_ref, w_refs[5], w_refs[6], win_ref, main0 + r0,
                                  dt_ref.at[rows])

    def rest_units():
        for r0 in subs:
            rows = pl.ds(r0, PROJ_SUB)
            yield from _proj_rows(hns[r0], cos_ref.at[rows], sin_ref.at[rows], *w_refs,
                                  *[o.at[rows] for o in (q_out, k_out, v_out, ga_out, z_out, qm_out, gm_out)])

    def decay_tables(k):
        rr = lax.broadcasted_iota(jnp.int32, (CHUNK, CHUNK), 0)
        cc = lax.broadcasted_iota(jnp.int32, (CHUNK, CHUNK), 1)
        fwd_col = cc < SSD_HEADS
        dt = _softplus(dt_ref[CHUNK * k:CHUNK * (k + 1), :] + dtb_ref[...])
        adt = dt * (a_ref[...] * LOG2E)
        cum = jnp.where(rr >= cc, 1.0, 0.0).astype(BF16)
        cum = jnp.concatenate([cum, jnp.ones((CHUNK, CHUNK), BF16)], axis=0)
        r3 = _dot(cum, jnp.concatenate(_split3(adt), axis=1))
        r3 = r3[:, 0:LANES] + r3[:, LANES:2 * LANES] + r3[:, 2 * LANES:]
        cs = r3[0:CHUNK, :]
        tl = r3[CHUNK:, :]
        zl = jnp.where(fwd_col, cs, cs - adt)
        ldt = jnp.log2(dt)
        rt = jnp.where(fwd_col, zl - ldt, zl + ldt).T
        rs = jnp.exp2(jnp.where(fwd_col, zl, tl - zl)).astype(BF16)
        sw = (jnp.exp2(jnp.where(fwd_col, tl - zl, zl)) * dt).astype(BF16)
        zl_out[CHUNK * k:CHUNK * (k + 1), :] = zl
        rt_out[k] = rt[0:N_DIR * SSD_HEADS, :]
        rs_out[CHUNK * k:CHUNK * (k + 1), :] = rs
        sw_out[CHUNK * k:CHUNK * (k + 1), :] = sw
        tl_out[k] = tl[0:TL_ROWS, :]
        return sw, tl[0:TL_ROWS, :]

    def conv_silu(k):
        ext = win_ref[CHUNK * k:CHUNK * (k + 2), :]
        pad = (SSD_CONV - 1) // 2
        centre = win_ref[main0 + CHUNK * k:main0 + CHUNK * (k + 1), :]
        acc = cb_ref[...] + cw_ref[pad:pad + 1, :] * centre.astype(F32)
        shifted = _dot(shift_ref[...], ext)
        tap = 0
        for jj in range(SSD_CONV):
            if jj == pad:
                continue
            acc = acc + cw_ref[jj:jj + 1, :] * shifted[CHUNK * tap:CHUNK * (tap + 1), :]
            tap += 1
        xc = _silu(acc)
        xc_out[CHUNK * k:CHUNK * (k + 1), :] = xc.astype(BF16)
        return xc

    order = list(reversed(range(cps)))
    tables, xcs = {}, {}

    def prepare(k):
        tables[k] = decay_tables(k)
        xcs[k] = conv_silu(k)

    def advance(k):
        sw, tl_part = tables[k]
        gb_out[k] = gb_ref[...].astype(BF16)
        (wexp,), dec = _ssd_expand([sw], tl_part, eb_ref[...])
        _ssd_state_update(gb_ref, xcs[k], wexp, dec)

    pieces = [functools.partial(prepare, k) for k in order] + [functools.partial(advance, k) for k in order]
    units = list(range(UNITS_PER_SUB * len(subs)))
    every = max(len(units) // (len(pieces) + 1), 1)
    stream = rest_units()
    for u in units:
        next(stream)
        if (u + 1) % every == 0 and pieces:
            pieces.pop(0)()
    for piece in pieces:
        piece()


def _proj_conv_cols(x_ref, halo, nw_ref, wxbc_ref, wdt_ref, win_ref, win_row, dt_out):
    def normed(rows_ref):
        xr = rows_ref[...]
        ms = jnp.mean(xr * xr, axis=-1, keepdims=True)
        return (xr * lax.rsqrt(ms + EPS) * nw_ref[...]).astype(BF16)

    hn = normed(x_ref)
    n_rows = hn.shape[0]
    if halo is None:
        lhs = hn
    else:
        xh_ref, at_seq_start = halo
        lhs = jnp.concatenate([normed(xh_ref), hn], axis=0)
    for j in range(XBC_W // GROUP_W):
        t = _dot(lhs, wxbc_ref[:, GROUP_W * j:GROUP_W * (j + 1)])
        cols = slice(GROUP_W * j, GROUP_W * (j + 1))
        if halo is not None:
            left = t[0:HALO, :]
            win_ref[win_row - HALO:win_row, cols] = jnp.where(at_seq_start, jnp.zeros_like(left), left).astype(BF16)
            t = t[HALO:, :]
        win_ref[win_row:win_row + n_rows, cols] = t.astype(BF16)
    dt_out[...] = _dot(hn, wdt_ref[...])
    return hn


def _proj_rows(hn, cos_ref, sin_ref,
               wq_ref, wk_ref, wv_ref, wga_ref, wz_ref, wxbc_ref, wdt_ref, wqm_ref, wgm_ref,
               q_out, k_out, v_out, ga_out, z_out, qm_out, gm_out):
    c = cos_ref[...]
    s = sin_ref[...]
    att_scale = ATT_HEAD_DIM ** -0.5 * LOG2E
    mem_scale = MEM_HEAD_DIM ** -0.5 * LOG2E

    for g in range(ATT_KV_HEADS):
        t = _dot(hn, wq_ref[:, GROUP_W * g:GROUP_W * (g + 1)])
        t1 = t[:, :LANES]
        t2 = t[:, LANES:]
        q_out[:, GROUP_W * g:GROUP_W * g + LANES] = ((t1 * c - t2 * s) * att_scale).astype(BF16)
        q_out[:, GROUP_W * g + LANES:GROUP_W * (g + 1)] = ((t1 * s + t2 * c) * att_scale).astype(BF16)
        yield

    t = _dot(hn, wk_ref[...])
    t1 = t[:, :LANES]
    t2 = t[:, LANES:]
    ko = (t1 * c - t2 * s, t1 * s + t2 * c)
    slot32 = lax.broadcasted_iota(jnp.int32, t1.shape, 1) // HALF
    for half in range(2):
        r = [ko[half]] + [pltpu.roll(ko[half], HALF * m, 1) for m in range(1, 4)]
        for g in range(ATT_KV_HEADS):
            rep = r[(3 - g) % 4]
            for j in (2, 1, 0):
                rep = jnp.where(slot32 == j, r[(j - g) % 4], rep)
            k_out[:, GROUP_W * g + LANES * half:GROUP_W * g + LANES * (half + 1)] = rep.astype(BF16)
    yield

    t = _dot(hn, wv_ref[...])
    low = lax.broadcasted_iota(jnp.int32, (t.shape[0], LANES), 1) < ATT_HEAD_DIM
    for pair in range(2):
        a = t[:, LANES * pair:LANES * (pair + 1)]
        ra = pltpu.roll(a, ATT_HEAD_DIM, 1)
        even = jnp.where(low, a, ra).astype(BF16)
        odd = jnp.where(low, ra, a).astype(BF16)
        for rep in range(2):
            v_out[:, GROUP_W * (2 * pair) + LANES * rep:GROUP_W * (2 * pair) + LANES * (rep + 1)] = even
            v_out[:, GROUP_W * (2 * pair + 1) + LANES * rep:GROUP_W * (2 * pair + 1) + LANES * (rep + 1)] = odd
    yield

    for j in range(ATT_W // GROUP_W):
        t = _dot(hn, wga_ref[:, GROUP_W * j:GROUP_W * (j + 1)])
        ga_out[:, GROUP_W * j:GROUP_W * (j + 1)] = _silu(t).astype(BF16)
        yield
    for j in range(SSD_W // GROUP_W):
        t = _dot(hn, wz_ref[:, GROUP_W * j:GROUP_W * (j + 1)])
        z_out[:, GROUP_W * j:GROUP_W * (j + 1)] = _silu(t).astype(BF16)
        yield
    for j in range(MEM_W // GROUP_W):
        t = _dot(hn, wqm_ref[:, GROUP_W * j:GROUP_W * (j + 1)])
        qm_out[:, GROUP_W * j:GROUP_W * (j + 1)] = (t * mem_scale).astype(BF16)
        yield
        t = _dot(hn, wgm_ref[:, GROUP_W * j:GROUP_W * (j + 1)])
        gm_out[:, GROUP_W * j:GROUP_W * (j + 1)] = _silu(t).astype(BF16)
        yield


def _ssd_right(xc_bf, zl, rt, rs, sw, tl_part, gb_next, zgate, dsk_ref, nw_ref, ef_ref, eb_ref, hf_ref, y_ref, rows):
    rr = lax.broadcasted_iota(jnp.int32, (CHUNK, CHUNK), 0)
    cc = lax.broadcasted_iota(jnp.int32, (CHUNK, CHUNK), 1)
    lower = rr >= cc
    upper = rr <= cc
    slot = lax.broadcasted_iota(jnp.int32, (CHUNK, GROUP_W), 1) // SSD_HEAD_DIM
    slotmask = [jnp.where(slot == j, 1.0, 0.0).astype(BF16) for j in range(SSD_HPG)]

    xc = xc_bf.astype(F32)
    scale_f = _dot(rs, ef_ref[...])
    scale_b = _dot(rs, eb_ref[...])
    for g in range(SSD_GROUPS):
        lo = GROUP_W * g
        xs = xc[:, lo:lo + GROUP_W]
        bm = xc_bf[:, SSD_W + SSD_STATE * g:SSD_W + SSD_STATE * (g + 1)]
        cm = xc_bf[:, SSD_W + SSD_GROUPS * SSD_STATE + SSD_STATE * g:
                   SSD_W + SSD_GROUPS * SSD_STATE + SSD_STATE * (g + 1)]
        gmat = _dot_nt(cm, bm)
        xs_bf = xc_bf[:, lo:lo + GROUP_W]
        m_parts = []
        x_parts = []
        for j in range(SSD_HPG):
            hf = SSD_HPG * g + j
            hb = SSD_HEADS + hf
            w_f = jnp.where(lower, jnp.exp2(zl[:, hf:hf + 1] - rt[hf:hf + 1, :]), 0.0)
            w_b = jnp.where(upper, jnp.exp2(rt[hb:hb + 1, :] - zl[:, hb:hb + 1]), 0.0)
            m_parts.append((gmat * (w_f + w_b)).astype(BF16))
            x_parts.append(xs_bf * slotmask[j])
        m_cat = jnp.concatenate(m_parts, axis=1)
        x_stack = jnp.concatenate(x_parts, axis=0)
        y = _dot(m_cat, x_stack)
        y = y + _dot(cm, hf_ref[g].astype(BF16)) * scale_f[:, lo:lo + GROUP_W]
        y = y + _dot(cm, gb_next[g]) * scale_b[:, lo:lo + GROUP_W]
        y = y + dsk_ref[:, lo:lo + GROUP_W] * xs
        gated = y * zgate[:, lo:lo + GROUP_W].astype(F32)
        ms = jnp.mean(gated * gated, axis=-1, keepdims=True)
        y_ref[rows, lo:lo + GROUP_W] = (gated * lax.rsqrt(ms + EPS) * nw_ref[:, lo:lo + GROUP_W]).astype(BF16)
    (wexp,), dec = _ssd_expand([sw], tl_part, ef_ref[...])
    _ssd_state_update(hf_ref, xc, wexp, dec)


def _ssd_tables():
    pad = (SSD_CONV - 1) // 2
    win_rows = 2 * CHUNK
    rows = np.arange(CHUNK)[:, None]
    cols = np.arange(win_rows)[None, :]
    shift = np.concatenate([(cols == rows + HALO - pad + j) for j in range(SSD_CONV) if j != pad]).astype(np.float32)
    hcol = np.arange(LANES)[:, None]
    hslot = (np.arange(SSD_W) // SSD_HEAD_DIM)[None, :]
    e_f = (hcol == hslot).astype(np.float32)
    e_b = (hcol == hslot + SSD_HEADS).astype(np.float32)
    return jnp.asarray(shift, BF16), jnp.asarray(e_f, BF16), jnp.asarray(e_b, BF16)


def _proj(x2, norm_w, cos128, sin128, weights, conv_w8, conv_b, dt_bias128, a128, shift, e_b, seq_len):
    t_rows = x2.shape[0]
    tm = PROJ_TM
    assert seq_len % tm == 0
    tiles_per_seq = seq_len // tm
    cps = tm // CHUNK
    halo_per_tile = tm // HALO
    win_rows = tm + CHUNK

    tile_of = lambda s: (s // tiles_per_seq) * tiles_per_seq + tiles_per_seq - 1 - s % tiles_per_seq
    row = lambda s: (tile_of(s), 0)
    row3 = lambda s: (tile_of(s), 0, 0)
    row4 = lambda s: (tile_of(s), 0, 0, 0)
    pos = lambda s: (tiles_per_seq - 1 - s % tiles_per_seq, 0)
    const = lambda s: (0, 0)
    resident = lambda a: pl.BlockSpec(a.shape, const, pipeline_mode=pl.Buffered(1))
    in_specs = [
        pl.BlockSpec((tm, D_MODEL), row),
        pl.BlockSpec((HALO, D_MODEL), lambda s: (jnp.maximum(tile_of(s) * halo_per_tile - 1, 0), 0)),
        pl.BlockSpec((1, D_MODEL), const),
        pl.BlockSpec((tm, LANES), pos),
        pl.BlockSpec((tm, LANES), pos),
    ] + [resident(w) for w in weights] + [
        pl.BlockSpec((8, XBC_W), const),
        pl.BlockSpec((1, XBC_W), const),
        pl.BlockSpec((1, LANES), const),
        pl.BlockSpec((1, LANES), const),
        resident(shift),
        resident(e_b),
    ]
    n_chunks = t_rows // CHUNK
    out_specs = [pl.BlockSpec((tm, w), row) for w in (ATT_W, ATT_W, ATT_W, ATT_W, SSD_W, MEM_W, MEM_W)] + [
        pl.BlockSpec((tm, XBC_W), row),
        pl.BlockSpec((tm, LANES), row),
        pl.BlockSpec((cps, N_DIR * SSD_HEADS, CHUNK), row3),
        pl.BlockSpec((tm, LANES), row),
        pl.BlockSpec((tm, LANES), row),
        pl.BlockSpec((cps, TL_ROWS, LANES), row3),
        pl.BlockSpec((cps, SSD_GROUPS, SSD_STATE, GROUP_W), row4),
    ]
    out_shape = [jax.ShapeDtypeStruct((t_rows, w), BF16) for w in (ATT_W, ATT_W, ATT_W, ATT_W, SSD_W, MEM_W, MEM_W)] + [
        jax.ShapeDtypeStruct((t_rows, XBC_W), BF16),
        jax.ShapeDtypeStruct((t_rows, LANES), F32),
        jax.ShapeDtypeStruct((n_chunks, N_DIR * SSD_HEADS, CHUNK), F32),
        jax.ShapeDtypeStruct((t_rows, LANES), BF16),
        jax.ShapeDtypeStruct((t_rows, LANES), BF16),
        jax.ShapeDtypeStruct((n_chunks, TL_ROWS, LANES), F32),
        jax.ShapeDtypeStruct((n_chunks, SSD_GROUPS, SSD_STATE, GROUP_W), BF16),
    ]
    return pl.pallas_call(
        functools.partial(_proj_kernel, tiles_per_seq=tiles_per_seq),
        grid=(t_rows // tm,),
        in_specs=in_specs,
        out_specs=out_specs,
        out_shape=out_shape,
        scratch_shapes=[
            pltpu.VMEM((win_rows, XBC_W), BF16),
            pltpu.VMEM((tm, LANES), F32),
            pltpu.VMEM((SSD_GROUPS, SSD_STATE, GROUP_W), F32),
        ],
        compiler_params=pltpu.CompilerParams(
            dimension_semantics=("arbitrary",), vmem_limit_bytes=VMEM_LIMIT),
        name="proj",
    )(x2, x2, norm_w.reshape(1, D_MODEL), cos128, sin128, *weights, conv_w8, conv_b, dt_bias128, a128, shift, e_b)


def _mix_kernel(*refs, n_tiles):
    p_ref = refs[-5]
    nblk = MIX_SUB // BLOCK
    pstride = 4 * BLOCK

    @pl.when(jnp.logical_and(pl.program_id(0) == 0, pl.program_id(1) == 0))
    def _():
        for blk in range(nblk):
            for p in range(nblk + 2):
                if p < blk or p > blk + 2:
                    p_ref[:, blk * BLOCK:(blk + 1) * BLOCK, pstride * p:pstride * (p + 1)] = jnp.zeros(
                        (ATT_KV_HEADS, BLOCK, pstride), BF16)

    for sub in range(refs[1].shape[1] // MIX_SUB):
        _mix_rows(sub, *refs, n_tiles=n_tiles)


def _mix_rows(sub, x_ref, q_ref, kl_ref, km_ref, kr_ref, vl_ref, vm_ref, vr_ref, ga_ref,
              qm_ref, gm_ref, mk_ref, mv_ref,
              xc_ref, zl_ref, rt_ref, rs_ref, sw_ref, tl_ref, gb_ref, zg_ref, dsk_ref, snw_ref, ef_ref, eb_ref,
              sink_ref, wo_ref, now_ref,
              o_ref, kst_ref, vst_ref, p_ref, att_ref, xat_ref, ssd_ref, hf_ref, *, n_tiles):
    i = pl.program_id(1)
    tq = MIX_SUB
    nsub = q_ref.shape[1] // tq
    nblk = tq // BLOCK
    rows = pl.ds(sub * tq, tq)

    if sub == 0:
        @pl.when(i == 0)
        def _():
            hf_ref[...] = jnp.zeros_like(hf_ref)
    for kk in range(tq // CHUNK):
        ck = sub * (tq // CHUNK) + kk
        crow = pl.ds(sub * tq + kk * CHUNK, CHUNK)
        _ssd_right(xc_ref[0, crow, :], zl_ref[crow, :], rt_ref[ck], rs_ref[crow, :], sw_ref[crow, :], tl_ref[ck],
                   gb_ref.at[ck], zg_ref[0, crow, :], dsk_ref, snw_ref, ef_ref, eb_ref, hf_ref, ssd_ref, crow)

    rr = lax.broadcasted_iota(jnp.int32, (BLOCK, WINDOW), 0)
    cc = lax.broadcasted_iota(jnp.int32, (BLOCK, WINDOW), 1)
    left_bias0 = jnp.where(cc >= rr, 0.0, NEG)
    right_bias0 = jnp.where(cc <= rr, 0.0, NEG)
    oslot = lax.broadcasted_iota(jnp.int32, (BLOCK, GROUP_W), 1) // ATT_HEAD_DIM
    pstride = 4 * BLOCK

    slot = lax.broadcasted_iota(jnp.int32, (BLOCK, GROUP_W), 1)
    kmask = [jnp.where(slot % LANES // HALF == j, 1.0, 0.0).astype(BF16) for j in range(4)]
    vmask = [jnp.where(slot // ATT_HEAD_DIM == j, 1.0, 0.0).astype(BF16) for j in range(4)]

    def piece(l_ref, m_ref, r_ref, p, lo):
        gp = sub * nblk + p
        if gp == 0:
            return l_ref[0, :, lo:lo + GROUP_W]
        if gp == nsub * nblk + 1:
            return r_ref[0, :, lo:lo + GROUP_W]
        return m_ref[0, (gp - 1) * BLOCK:gp * BLOCK, lo:lo + GROUP_W]

    def build_stacks(g):
        lo = GROUP_W * g
        for p in range(nblk + 2):
            kp = piece(kl_ref, km_ref, kr_ref, p, lo)
            vp = piece(vl_ref, vm_ref, vr_ref, p, lo)
            for j in range(4):
                rows = pl.ds(pstride * p + BLOCK * j, BLOCK)
                kst_ref[g, rows, :] = kp * kmask[j]
                vst_ref[g, rows, :] = vp * vmask[j]

    def scores(g):
        return _dot_nt(q_ref[0, rows, GROUP_W * g:GROUP_W * (g + 1)], kst_ref[g])

    def finish(g, s_all):
        lo = GROUP_W * g
        inv_rows = []
        for blk in range(nblk):
            r0 = blk * BLOCK
            left_dead = jnp.logical_and(i == 0, sub == 0 and blk == 0)
            right_dead = jnp.logical_and(i == n_tiles - 1, sub == nsub - 1 and blk == nblk - 1)
            left_bias = left_bias0 + jnp.where(left_dead, NEG, 0.0)
            right_bias = right_bias0 + jnp.where(right_dead, NEG, 0.0)
            inv = jnp.zeros((BLOCK, GROUP_W), F32)
            for j in range(4):
                c_l = pstride * blk + BLOCK * j
                c_m = c_l + pstride
                c_r = c_m + pstride
                s_l = s_all[r0:r0 + BLOCK, c_l:c_l + BLOCK] + left_bias
                s_m = s_all[r0:r0 + BLOCK, c_m:c_m + BLOCK]
                s_r = s_all[r0:r0 + BLOCK, c_r:c_r + BLOCK] + right_bias
                sink = sink_ref[0, 4 * g + j] * LOG2E
                m = jnp.maximum(jnp.max(jnp.maximum(jnp.maximum(s_l, s_m), s_r), axis=-1, keepdims=True), sink)
                p_l = jnp.exp2(s_l - m)
                p_m = jnp.exp2(s_m - m)
                p_r = jnp.exp2(s_r - m)
                den = jnp.sum(p_l + p_m + p_r, axis=-1, keepdims=True) + jnp.exp2(sink - m)
                p_ref[g, r0:r0 + BLOCK, c_l:c_l + BLOCK] = p_l.astype(BF16)
                p_ref[g, r0:r0 + BLOCK, c_m:c_m + BLOCK] = p_m.astype(BF16)
                p_ref[g, r0:r0 + BLOCK, c_r:c_r + BLOCK] = p_r.astype(BF16)
                inv = jnp.where(oslot == j, jnp.broadcast_to(1.0 / den, inv.shape), inv)
            inv_rows.append(inv)
        o = _dot(p_ref[g], vst_ref[g]) * jnp.concatenate(inv_rows, axis=0)
        gate = ga_ref[0, rows, lo:lo + GROUP_W].astype(F32)
        att_ref[rows, lo:lo + GROUP_W] = (o * gate).astype(BF16)

    build_stacks(0)
    s_next = scores(0)
    for g in range(ATT_KV_HEADS):
        s_cur = s_next
        if g + 1 < ATT_KV_HEADS:
            build_stacks(g + 1)
            s_next = scores(g + 1)
        finish(g, s_cur)

    for h in range(MEM_HEADS):
        lo = MEM_HEAD_DIM * h
        s = _dot_nt(qm_ref[0, rows, lo:lo + MEM_HEAD_DIM], mk_ref[0, :, lo:lo + MEM_HEAD_DIM])
        m = jnp.max(s, axis=-1, keepdims=True)
        p = jnp.exp2(s - m)
        den = jnp.sum(p, axis=-1, keepdims=True)
        o = _dot(p.astype(BF16), mv_ref[0, :, lo:lo + MEM_HEAD_DIM]) * (1.0 / den)
        xat_ref[rows, lo:lo + MEM_HEAD_DIM] = (o * gm_ref[0, rows, lo:lo + MEM_HEAD_DIM].astype(F32)).astype(BF16)

    delta = _dot(att_ref[rows, :], wo_ref[0:ATT_W, :])
    delta = delta + _dot(ssd_ref[rows, :], wo_ref[ATT_W:ATT_W + SSD_W, :])
    delta = delta + _dot(xat_ref[rows, :], wo_ref[ATT_W + SSD_W:, :])
    hres = x_ref[0, rows, :] + delta
    ms = jnp.mean(hres * hres, axis=-1, keepdims=True)
    o_ref[0, rows, :] = hres * lax.rsqrt(ms + EPS) * now_ref[...]


def _mix(x, q, k, v, ga, qm, gm, mk, mv, xc, zl, rt, rs, sw, tl, gb, zs, dskip, ssd_norm_w, e_f, e_b,
         sink, w_out_bf, norm_out_w):
    b, seq_len, _ = x.shape
    tq = MIX_TQ
    n_tiles = seq_len // tq
    bpt = tq // BLOCK
    sub_blocks = MIX_SUB // BLOCK
    n_blocks = seq_len // BLOCK
    span = BLOCK + 2 * WINDOW
    main = lambda w: pl.BlockSpec((1, tq, w), lambda bi, i: (bi, i, 0))
    left = lambda w: pl.BlockSpec((1, BLOCK, w), lambda bi, i: (bi, jnp.maximum(i * bpt - 1, 0), 0))
    right = lambda w: pl.BlockSpec((1, BLOCK, w), lambda bi, i: (bi, jnp.minimum((i + 1) * bpt, n_blocks - 1), 0))
    per_b = lambda r, w: pl.BlockSpec((1, r, w), lambda bi, i: (bi, 0, 0))
    const2 = lambda bi, i: (0, 0)
    cpt = tq // CHUNK
    tile_rows = pl.BlockSpec((tq, LANES), lambda bi, i: (bi * n_tiles + i, 0))
    per_chunk = lambda r, w: pl.BlockSpec((cpt, r, w), lambda bi, i: (bi * n_tiles + i, 0, 0))
    in_specs = [
        main(D_MODEL), main(ATT_W),
        left(ATT_W), main(ATT_W), right(ATT_W),
        left(ATT_W), main(ATT_W), right(ATT_W),
        main(ATT_W), main(MEM_W), main(MEM_W),
        per_b(MEM_LEN, MEM_W), per_b(MEM_LEN, MEM_W),
        main(XBC_W), tile_rows, per_chunk(N_DIR * SSD_HEADS, CHUNK), tile_rows, tile_rows,
        per_chunk(TL_ROWS, LANES),
        pl.BlockSpec((cpt, SSD_GROUPS, SSD_STATE, GROUP_W), lambda bi, i: (bi * n_tiles + i, 0, 0, 0)),
        main(SSD_W),
        pl.BlockSpec((1, SSD_W), const2), pl.BlockSpec((1, SSD_W), const2),
        pl.BlockSpec((LANES, SSD_W), const2), pl.BlockSpec((LANES, SSD_W), const2),
        pl.BlockSpec(memory_space=pltpu.SMEM),
        pl.BlockSpec((MIX_W, D_MODEL), const2),
        pl.BlockSpec((1, D_MODEL), const2),
    ]
    return pl.pallas_call(
        functools.partial(_mix_kernel, n_tiles=n_tiles),
        grid=(b, n_tiles),
        in_specs=in_specs,
        out_specs=pl.BlockSpec((1, tq, D_MODEL), lambda bi, i: (bi, i, 0)),
        out_shape=jax.ShapeDtypeStruct((b, seq_len, D_MODEL), F32),
        scratch_shapes=[
            pltpu.VMEM((ATT_KV_HEADS, (sub_blocks + 2) * 4 * BLOCK, GROUP_W), BF16),
            pltpu.VMEM((ATT_KV_HEADS, (sub_blocks + 2) * 4 * BLOCK, GROUP_W), BF16),
            pltpu.VMEM((ATT_KV_HEADS, MIX_SUB, (sub_blocks + 2) * 4 * BLOCK), BF16),
            pltpu.VMEM((tq, ATT_W), BF16),
            pltpu.VMEM((tq, MEM_W), BF16),
            pltpu.VMEM((tq, SSD_W), BF16),
            pltpu.VMEM((SSD_GROUPS, SSD_STATE, GROUP_W), F32),
        ],
        compiler_params=pltpu.CompilerParams(
            dimension_semantics=("arbitrary", "arbitrary"), vmem_limit_bytes=VMEM_LIMIT),
        name="mix",
    )(x, q, k, k, k, v, v, v, ga, qm, gm, mk, mv, xc, zl, rt, rs, sw, tl, gb, zs, dskip, ssd_norm_w, e_f, e_b,
      sink, w_out_bf, norm_out_w.reshape(1, D_MODEL))


def _q_perm():
    idx = np.zeros((ATT_W,), np.int32)
    for g in range(ATT_KV_HEADS):
        for half in range(2):
            for j in range(4):
                for d in range(HALF):
                    idx[GROUP_W * g + LANES * half + HALF * j + d] = (4 * g + j) * ATT_HEAD_DIM + HALF * half + d
    return idx


def _k_perm():
    idx = np.zeros((KV_W,), np.int32)
    for half in range(2):
        for g in range(ATT_KV_HEADS):
            for d in range(HALF):
                idx[LANES * half + HALF * g + d] = g * ATT_HEAD_DIM + HALF * half + d
    return idx


def kernel(x, mem, norm_mem_w, norm_in_w, w_in, attn_sink, conv_w, conv_b, dt_bias, a_log, d_skip,
           ssd_norm_w, w_mem_kv, w_out, norm_out_w):
    b, seq_len, _ = x.shape
    depth = w_in.shape[0]

    pos = jnp.arange(seq_len, dtype=F32)
    inv_freq = 1.0 / (ROPE_THETA ** (jnp.arange(0, ATT_HEAD_DIM, 2, dtype=F32) / ATT_HEAD_DIM))
    ang = pos[:, None] * inv_freq[None, :]
    cos128 = jnp.tile(jnp.cos(ang), (1, 4))
    sin128 = jnp.tile(jnp.sin(ang), (1, 4))

    assert depth == 1, "the final norm is fused into the single layer's output stage"
    layer = 0
    w = w_in[layer]
    o = 0
    parts = []
    for width in (ATT_W, KV_W, KV_W, ATT_W, SSD_W, XBC_W, N_DIR * SSD_HEADS, MEM_W, MEM_W):
        parts.append(w[:, o:o + width])
        o += width
    wq, wk, wv, wga, wz, wxbc, wdt, wqm, wgm = parts
    wq = wq[:, _q_perm()]
    wk = wk[:, _k_perm()]
    pad16 = LANES - N_DIR * SSD_HEADS
    wdt = jnp.pad(wdt, ((0, 0), (0, pad16)))
    weights = [m.astype(BF16) for m in (wq, wk, wv, wga, wz, wxbc, wdt, wqm, wgm)]

    mk, mv = _memkv(mem, norm_mem_w, w_mem_kv[layer].astype(BF16))
    dt_bias128 = jnp.pad(dt_bias[layer].reshape(1, -1).astype(F32), ((0, 0), (0, pad16)))
    a128 = jnp.pad(-jnp.exp(a_log[layer].astype(F32)).reshape(1, -1), ((0, 0), (0, pad16)))
    dskip512 = jnp.repeat(d_skip[layer].astype(F32), SSD_HEAD_DIM).reshape(1, SSD_W)
    conv_w8 = jnp.pad(conv_w[layer].astype(F32), ((0, 8 - SSD_CONV), (0, 0)))
    shift, e_f, e_b = _ssd_tables()
    q, k, v, ga, zs, qm, gm, xc, zl, rt, rs, sw, tl, gb = _proj(
        x.reshape(b * seq_len, D_MODEL), norm_in_w[layer], cos128, sin128, weights, conv_w8,
        conv_b[layer].reshape(1, XBC_W).astype(F32), dt_bias128, a128, shift, e_b, seq_len)

    r3 = lambda arr: arr.reshape(b, seq_len, arr.shape[-1])
    return _mix(x, r3(q), r3(k), r3(v), r3(ga), r3(qm), r3(gm), mk, mv, r3(xc), zl, rt, rs, sw, tl, gb, r3(zs),
                dskip512, ssd_norm_w[layer].reshape(1, SSD_W).astype(F32), e_f, e_b,
                attn_sink[layer].reshape(1, ATT_HEADS).astype(F32), w_out[layer].astype(BF16), norm_out_w)
```

```python
import functools
import math

import jax
import jax.numpy as jnp
import numpy as np
from jax import lax
from jax.experimental import pallas as pl
from jax.experimental.pallas import tpu as pltpu

F32 = jnp.float32
BF16 = jnp.bfloat16

D_MODEL = 1024
ATT_HEADS = 16
ATT_KV_HEADS = 4
ATT_HEAD_DIM = 64
HALF = ATT_HEAD_DIM // 2
ATT_W = ATT_HEADS * ATT_HEAD_DIM
KV_W = ATT_KV_HEADS * ATT_HEAD_DIM
WINDOW = 128
BLOCK = 128
ROPE_THETA = 10000.0
SSD_HEADS = 8
SSD_HEAD_DIM = 64
SSD_W = SSD_HEADS * SSD_HEAD_DIM
SSD_GROUPS = 2
SSD_HPG = SSD_HEADS // SSD_GROUPS
SSD_STATE = 128
SSD_CONV = 5
CHUNK = 128
N_DIR = 2
XBC_W = SSD_W + 2 * SSD_GROUPS * SSD_STATE
MEM_LEN = 256
MEM_HEADS = 4
MEM_HEAD_DIM = 128
MEM_W = MEM_HEADS * MEM_HEAD_DIM
MIX_W = ATT_W + SSD_W + MEM_W
EPS = 1e-6

LANES = 128
GROUP_W = 256
HALO = 16
NEG = -1e30
VMEM_LIMIT = 56 * 1024 * 1024
LOG2E = math.log2(math.e)

PROJ_TM = 512
PROJ_SUB = 256
MIX_TQ = 512
MIX_SUB = 256
SSD_CPS = 4
UNITS_PER_SUB = 16


def _silu(v):
    return v * (1.0 / (1.0 + jnp.exp2(v * (-LOG2E))))


def _softplus(v):
    return jnp.maximum(v, 0.0) + jnp.log(1.0 + jnp.exp(-jnp.abs(v)))


def _dot(a, b):
    return jnp.dot(a, b, preferred_element_type=F32)


def _dot_nt(a, b):
    return lax.dot_general(a, b, (((1,), (1,)), ((), ())), preferred_element_type=F32)


def _split3(v):
    hi = v.astype(BF16)
    r1 = v - hi.astype(F32)
    mid = r1.astype(BF16)
    lo = (r1 - mid.astype(F32)).astype(BF16)
    return hi, mid, lo


def _memkv_kernel(mem_ref, nw_ref, w_ref, mk_ref, mv_ref):
    m = mem_ref[0]
    ms = jnp.mean(m * m, axis=-1, keepdims=True)
    mn = (m * lax.rsqrt(ms + EPS) * nw_ref[...]).astype(BF16)
    mk_ref[0] = _dot(mn, w_ref[:, :MEM_W]).astype(BF16)
    mv_ref[0] = _dot(mn, w_ref[:, MEM_W:]).astype(BF16)


def _memkv(mem, norm_mem_w, w_mem_kv_bf):
    b = mem.shape[0]
    return pl.pallas_call(
        _memkv_kernel,
        grid=(b,),
        in_specs=[
            pl.BlockSpec((1, MEM_LEN, D_MODEL), lambda i: (i, 0, 0)),
            pl.BlockSpec((1, D_MODEL), lambda i: (0, 0)),
            pl.BlockSpec((D_MODEL, 2 * MEM_W), lambda i: (0, 0)),
        ],
        out_specs=[
            pl.BlockSpec((1, MEM_LEN, MEM_W), lambda i: (i, 0, 0)),
            pl.BlockSpec((1, MEM_LEN, MEM_W), lambda i: (i, 0, 0)),
        ],
        out_shape=[jax.ShapeDtypeStruct((b, MEM_LEN, MEM_W), BF16)] * 2,
        compiler_params=pltpu.CompilerParams(
            dimension_semantics=("arbitrary",), vmem_limit_bytes=VMEM_LIMIT),
        name="memkv",
    )(mem, norm_mem_w.reshape(1, D_MODEL), w_mem_kv_bf)


TL_ROWS = HALO
N_PROJ_W = 9


def _ssd_expand(mats, tl_part, e):
    lhs = jnp.concatenate(list(mats) + list(_split3(tl_part)), axis=0)
    r = _dot(lhs, e)
    n = len(mats) * CHUNK
    outs = [r[CHUNK * m:CHUNK * (m + 1), :] for m in range(len(mats))]
    dec = jnp.exp2(r[n:n + TL_ROWS, :] + r[n + TL_ROWS:n + 2 * TL_ROWS, :] + r[n + 2 * TL_ROWS:n + 3 * TL_ROWS, :])
    return outs, dec


def _ssd_state_update(st_ref, xc, wexp, dec):
    for g in range(SSD_GROUPS):
        lo = GROUP_W * g
        xs = xc[:, lo:lo + GROUP_W]
        bm = xc[:, SSD_W + SSD_STATE * g:SSD_W + SSD_STATE * (g + 1)]
        xw = (xs * wexp[:, lo:lo + GROUP_W]).astype(BF16)
        upd = _dot(bm.T.astype(BF16), xw)
        st_ref[g] = st_ref[g] * dec[0:1, lo:lo + GROUP_W] + upd


def _proj_kernel(x_ref, xh_ref, nw_ref, cos_ref, sin_ref, *refs, tiles_per_seq):
    w_refs = refs[:N_PROJ_W]
    cw_ref, cb_ref, dtb_ref, a_ref, eb_ref = refs[N_PROJ_W:N_PROJ_W + 5]
    outs = refs[N_PROJ_W + 5:]
    q_out, k_out, v_out, ga_out, z_out, qm_out, gm_out = outs[:7]
    xc_out, zl_out, rt_out, rs_out, sw_out, tl_out, gb_out = outs[7:14]
    win_ref, dt_ref, gb_ref = outs[14:]

    tm = x_ref.shape[0]
    cps = tm // CHUNK
    j = tiles_per_seq - 1 - pl.program_id(0) % tiles_per_seq
    first_step = j == tiles_per_seq - 1
    main0 = HALO

    @pl.when(first_step)
    def _():
        gb_ref[...] = jnp.zeros_like(gb_ref)

    prev_head = win_ref[main0:main0 + HALO, :]
    win_ref[main0 + tm:main0 + tm + HALO, :] = jnp.where(first_step, jnp.zeros_like(prev_head), prev_head)

    subs = list(reversed(range(0, tm, PROJ_SUB)))
    hns = {}
    for r0 in subs:
        rows = pl.ds(r0, PROJ_SUB)
        halo = (xh_ref, j == 0) if r0 == 0 else None
        hns[r0] = _proj_conv_cols(x_ref.at[rows], halo, nw_ref, w_refs[5], w_refs[6], win_ref, main0 + r0,
                                  dt_ref.at[rows])

    def rest_units():
        for r0 in subs:
            rows = pl.ds(r0, PROJ_SUB)
            yield from _proj_rows(hns[r0], cos_ref.at[rows], sin_ref.at[rows], *w_refs,
                                  *[o.at[rows] for o in (q_out, k_out, v_out, ga_out, z_out, qm_out, gm_out)])

    def decay_tables(k):
        rr = lax.broadcasted_iota(jnp.int32, (CHUNK, CHUNK), 0)
        cc = lax.broadcasted_iota(jnp.int32, (CHUNK, CHUNK), 1)
        fwd_col = cc < SSD_HEADS
        dt = _softplus(dt_ref[CHUNK * k:CHUNK * (k + 1), :] + dtb_ref[...])
        adt = dt * (a_ref[...] * LOG2E)
        cum = jnp.where(rr >= cc, 1.0, 0.0).astype(BF16)
        cum = jnp.concatenate([cum, jnp.ones((CHUNK, CHUNK), BF16)], axis=0)
        r3 = _dot(cum, jnp.concatenate(_split3(adt), axis=1))
        r3 = r3[:, 0:LANES] + r3[:, LANES:2 * LANES] + r3[:, 2 * LANES:]
        cs = r3[0:CHUNK, :]
        tl = r3[CHUNK:, :]
        zl = jnp.where(fwd_col, cs, cs - adt)
        ldt = jnp.log2(dt)
        rt = jnp.where(fwd_col, zl - ldt, zl + ldt).T
        rs = jnp.exp2(jnp.where(fwd_col, zl, tl - zl)).astype(BF16)
        sw = (jnp.exp2(jnp.where(fwd_col, tl - zl, zl)) * dt).astype(BF16)
        zl_out[CHUNK * k:CHUNK * (k + 1), :] = zl
        rt_out[k] = rt[0:N_DIR * SSD_HEADS, :]
        rs_out[CHUNK * k:CHUNK * (k + 1), :] = rs
        sw_out[CHUNK * k:CHUNK * (k + 1), :] = sw
        tl_out[k] = tl[0:TL_ROWS, :]
        return sw, tl[0:TL_ROWS, :]

    def conv_silu(k):
        pad = (SSD_CONV - 1) // 2
        acc = cb_ref[...]
        for jj in range(SSD_CONV):
            acc = acc + cw_ref[jj:jj + 1, :] * win_ref[pl.ds(main0 + CHUNK * k - pad + jj, CHUNK), :]
        xc = _silu(acc)
        xc_out[CHUNK * k:CHUNK * (k + 1), :] = xc.astype(BF16)
        return xc

    order = list(reversed(range(cps)))
    tables, xcs = {}, {}

    def prepare(k):
        tables[k] = decay_tables(k)
        xcs[k] = conv_silu(k)

    def advance(k):
        sw, tl_part = tables[k]
        gb_out[k] = gb_ref[...].astype(BF16)
        (wexp,), dec = _ssd_expand([sw], tl_part, eb_ref[...])
        _ssd_state_update(gb_ref, xcs[k], wexp, dec)

    pieces = [functools.partial(prepare, k) for k in order] + [functools.partial(advance, k) for k in order]
    units = list(range(UNITS_PER_SUB * len(subs)))
    every = max(len(units) // (len(pieces) + 1), 1)
    stream = rest_units()
    for u in units:
        next(stream)
        if (u + 1) % every == 0 and pieces:
            pieces.pop(0)()
    for piece in pieces:
        piece()


def _proj_conv_cols(x_ref, halo, nw_ref, wxbc_ref, wdt_ref, win_ref, win_row, dt_out):
    def normed(rows_ref):
        xr = rows_ref[...]
        ms = jnp.mean(xr * xr, axis=-1, keepdims=True)
        return (xr * lax.rsqrt(ms + EPS) * nw_ref[...]).astype(BF16)

    hn = normed(x_ref)
    n_rows = hn.shape[0]
    if halo is None:
        lhs = hn
    else:
        xh_ref, at_seq_start = halo
        lhs = jnp.concatenate([normed(xh_ref), hn], axis=0)
    for j in range(XBC_W // GROUP_W):
        t = _dot(lhs, wxbc_ref[:, GROUP_W * j:GROUP_W * (j + 1)])
        cols = slice(GROUP_W * j, GROUP_W * (j + 1))
        if halo is not None:
            left = t[0:HALO, :]
            win_ref[win_row - HALO:win_row, cols] = jnp.where(at_seq_start, jnp.zeros_like(left), left)
            t = t[HALO:, :]
        win_ref[win_row:win_row + n_rows, cols] = t
    dt_out[...] = _dot(hn, wdt_ref[...])
    return hn


def _proj_rows(hn, cos_ref, sin_ref,
               wq_ref, wk_ref, wv_ref, wga_ref, wz_ref, wxbc_ref, wdt_ref, wqm_ref, wgm_ref,
               q_out, k_out, v_out, ga_out, z_out, qm_out, gm_out):
    c = cos_ref[...]
    s = sin_ref[...]
    att_scale = ATT_HEAD_DIM ** -0.5 * LOG2E
    mem_scale = MEM_HEAD_DIM ** -0.5 * LOG2E

    for g in range(ATT_KV_HEADS):
        t = _dot(hn, wq_ref[:, GROUP_W * g:GROUP_W * (g + 1)])
        t1 = t[:, :LANES]
        t2 = t[:, LANES:]
        q_out[:, GROUP_W * g:GROUP_W * g + LANES] = ((t1 * c - t2 * s) * att_scale).astype(BF16)
        q_out[:, GROUP_W * g + LANES:GROUP_W * (g + 1)] = ((t1 * s + t2 * c) * att_scale).astype(BF16)
        yield

    t = _dot(hn, wk_ref[...])
    t1 = t[:, :LANES]
    t2 = t[:, LANES:]
    ko = (t1 * c - t2 * s, t1 * s + t2 * c)
    slot32 = lax.broadcasted_iota(jnp.int32, t1.shape, 1) // HALF
    for half in range(2):
        r = [ko[half]] + [pltpu.roll(ko[half], HALF * m, 1) for m in range(1, 4)]
        for g in range(ATT_KV_HEADS):
            rep = r[(3 - g) % 4]
            for j in (2, 1, 0):
                rep = jnp.where(slot32 == j, r[(j - g) % 4], rep)
            k_out[:, GROUP_W * g + LANES * half:GROUP_W * g + LANES * (half + 1)] = rep.astype(BF16)
    yield

    t = _dot(hn, wv_ref[...])
    low = lax.broadcasted_iota(jnp.int32, (t.shape[0], LANES), 1) < ATT_HEAD_DIM
    for pair in range(2):
        a = t[:, LANES * pair:LANES * (pair + 1)]
        ra = pltpu.roll(a, ATT_HEAD_DIM, 1)
        even = jnp.where(low, a, ra).astype(BF16)
        odd = jnp.where(low, ra, a).astype(BF16)
        for rep in range(2):
            v_out[:, GROUP_W * (2 * pair) + LANES * rep:GROUP_W * (2 * pair) + LANES * (rep + 1)] = even
            v_out[:, GROUP_W * (2 * pair + 1) + LANES * rep:GROUP_W * (2 * pair + 1) + LANES * (rep + 1)] = odd
    yield

    for j in range(ATT_W // GROUP_W):
        t = _dot(hn, wga_ref[:, GROUP_W * j:GROUP_W * (j + 1)])
        ga_out[:, GROUP_W * j:GROUP_W * (j + 1)] = _silu(t).astype(BF16)
        yield
    for j in range(SSD_W // GROUP_W):
        t = _dot(hn, wz_ref[:, GROUP_W * j:GROUP_W * (j + 1)])
        z_out[:, GROUP_W * j:GROUP_W * (j + 1)] = _silu(t).astype(BF16)
        yield
    for j in range(MEM_W // GROUP_W):
        t = _dot(hn, wqm_ref[:, GROUP_W * j:GROUP_W * (j + 1)])
        qm_out[:, GROUP_W * j:GROUP_W * (j + 1)] = (t * mem_scale).astype(BF16)
        yield
        t = _dot(hn, wgm_ref[:, GROUP_W * j:GROUP_W * (j + 1)])
        gm_out[:, GROUP_W * j:GROUP_W * (j + 1)] = _silu(t).astype(BF16)
        yield


def _ssd_right(xc_bf, zl, rt, rs, sw, tl_part, gb_next, zgate, dsk_ref, nw_ref, ef_ref, eb_ref, hf_ref, y_ref, rows):
    rr = lax.broadcasted_iota(jnp.int32, (CHUNK, CHUNK), 0)
    cc = lax.broadcasted_iota(jnp.int32, (CHUNK, CHUNK), 1)
    lower = rr >= cc
    upper = rr <= cc
    slot = lax.broadcasted_iota(jnp.int32, (CHUNK, GROUP_W), 1) // SSD_HEAD_DIM
    slotmask = [jnp.where(slot == j, 1.0, 0.0).astype(BF16) for j in range(SSD_HPG)]

    xc = xc_bf.astype(F32)
    scale_f = _dot(rs, ef_ref[...])
    scale_b = _dot(rs, eb_ref[...])
    for g in range(SSD_GROUPS):
        lo = GROUP_W * g
        xs = xc[:, lo:lo + GROUP_W]
        bm = xc_bf[:, SSD_W + SSD_STATE * g:SSD_W + SSD_STATE * (g + 1)]
        cm = xc_bf[:, SSD_W + SSD_GROUPS * SSD_STATE + SSD_STATE * g:
                   SSD_W + SSD_GROUPS * SSD_STATE + SSD_STATE * (g + 1)]
        gmat = _dot_nt(cm, bm)
        xs_bf = xc_bf[:, lo:lo + GROUP_W]
        m_parts = []
        x_parts = []
        for j in range(SSD_HPG):
            hf = SSD_HPG * g + j
            hb = SSD_HEADS + hf
            w_f = jnp.where(lower, jnp.exp2(zl[:, hf:hf + 1] - rt[hf:hf + 1, :]), 0.0)
            w_b = jnp.where(upper, jnp.exp2(rt[hb:hb + 1, :] - zl[:, hb:hb + 1]), 0.0)
            m_parts.append((gmat * (w_f + w_b)).astype(BF16))
            x_parts.append(xs_bf * slotmask[j])
        m_cat = jnp.concatenate(m_parts, axis=1)
        x_stack = jnp.concatenate(x_parts, axis=0)
        y = _dot(m_cat, x_stack)
        y = y + _dot(cm, hf_ref[g].astype(BF16)) * scale_f[:, lo:lo + GROUP_W]
        y = y + _dot(cm, gb_next[g]) * scale_b[:, lo:lo + GROUP_W]
        y = y + dsk_ref[:, lo:lo + GROUP_W] * xs
        gated = y * zgate[:, lo:lo + GROUP_W].astype(F32)
        ms = jnp.mean(gated * gated, axis=-1, keepdims=True)
        y_ref[rows, lo:lo + GROUP_W] = (gated * lax.rsqrt(ms + EPS) * nw_ref[:, lo:lo + GROUP_W]).astype(BF16)
    (wexp,), dec = _ssd_expand([sw], tl_part, ef_ref[...])
    _ssd_state_update(hf_ref, xc, wexp, dec)


def _ssd_tables():
    hcol = np.arange(LANES)[:, None]
    hslot = (np.arange(SSD_W) // SSD_HEAD_DIM)[None, :]
    e_f = (hcol == hslot).astype(np.float32)
    e_b = (hcol == hslot + SSD_HEADS).astype(np.float32)
    return jnp.asarray(e_f, BF16), jnp.asarray(e_b, BF16)


def _proj(x2, norm_w, cos128, sin128, weights, conv_w8, conv_b, dt_bias128, a128, e_b, seq_len):
    t_rows = x2.shape[0]
    tm = PROJ_TM
    assert seq_len % tm == 0
    tiles_per_seq = seq_len // tm
    cps = tm // CHUNK
    halo_per_tile = tm // HALO
    win_rows = tm + 2 * HALO

    tile_of = lambda s: (s // tiles_per_seq) * tiles_per_seq + tiles_per_seq - 1 - s % tiles_per_seq
    row = lambda s: (tile_of(s), 0)
    row3 = lambda s: (tile_of(s), 0, 0)
    row4 = lambda s: (tile_of(s), 0, 0, 0)
    pos = lambda s: (tiles_per_seq - 1 - s % tiles_per_seq, 0)
    const = lambda s: (0, 0)
    resident = lambda a: pl.BlockSpec(a.shape, const, pipeline_mode=pl.Buffered(1))
    in_specs = [
        pl.BlockSpec((tm, D_MODEL), row),
        pl.BlockSpec((HALO, D_MODEL), lambda s: (jnp.maximum(tile_of(s) * halo_per_tile - 1, 0), 0)),
        pl.BlockSpec((1, D_MODEL), const),
        pl.BlockSpec((tm, LANES), pos),
        pl.BlockSpec((tm, LANES), pos),
    ] + [resident(w) for w in weights] + [
        pl.BlockSpec((8, XBC_W), const),
        pl.BlockSpec((1, XBC_W), const),
        pl.BlockSpec((1, LANES), const),
        pl.BlockSpec((1, LANES), const),
        resident(e_b),
    ]
    n_chunks = t_rows // CHUNK
    out_specs = [pl.BlockSpec((tm, w), row) for w in (ATT_W, ATT_W, ATT_W, ATT_W, SSD_W, MEM_W, MEM_W)] + [
        pl.BlockSpec((tm, XBC_W), row),
        pl.BlockSpec((tm, LANES), row),
        pl.BlockSpec((cps, N_DIR * SSD_HEADS, CHUNK), row3),
        pl.BlockSpec((tm, LANES), row),
        pl.BlockSpec((tm, LANES), row),
        pl.BlockSpec((cps, TL_ROWS, LANES), row3),
        pl.BlockSpec((cps, SSD_GROUPS, SSD_STATE, GROUP_W), row4),
    ]
    out_shape = [jax.ShapeDtypeStruct((t_rows, w), BF16) for w in (ATT_W, ATT_W, ATT_W, ATT_W, SSD_W, MEM_W, MEM_W)] + [
        jax.ShapeDtypeStruct((t_rows, XBC_W), BF16),
        jax.ShapeDtypeStruct((t_rows, LANES), F32),
        jax.ShapeDtypeStruct((n_chunks, N_DIR * SSD_HEADS, CHUNK), F32),
        jax.ShapeDtypeStruct((t_rows, LANES), BF16),
        jax.ShapeDtypeStruct((t_rows, LANES), BF16),
        jax.ShapeDtypeStruct((n_chunks, TL_ROWS, LANES), F32),
        jax.ShapeDtypeStruct((n_chunks, SSD_GROUPS, SSD_STATE, GROUP_W), BF16),
    ]
    return pl.pallas_call(
        functools.partial(_proj_kernel, tiles_per_seq=tiles_per_seq),
        grid=(t_rows // tm,),
        in_specs=in_specs,
        out_specs=out_specs,
        out_shape=out_shape,
        scratch_shapes=[
            pltpu.VMEM((win_rows, XBC_W), F32),
            pltpu.VMEM((tm, LANES), F32),
            pltpu.VMEM((SSD_GROUPS, SSD_STATE, GROUP_W), F32),
        ],
        compiler_params=pltpu.CompilerParams(
            dimension_semantics=("arbitrary",), vmem_limit_bytes=VMEM_LIMIT),
        name="proj",
    )(x2, x2, norm_w.reshape(1, D_MODEL), cos128, sin128, *weights, conv_w8, conv_b, dt_bias128, a128, e_b)


def _mix_kernel(*refs, n_tiles):
    p_ref = refs[-5]
    nblk = MIX_SUB // BLOCK
    pstride = 4 * BLOCK

    @pl.when(jnp.logical_and(pl.program_id(0) == 0, pl.program_id(1) == 0))
    def _():
        for blk in range(nblk):
            for p in range(nblk + 2):
                if p < blk or p > blk + 2:
                    p_ref[:, blk * BLOCK:(blk + 1) * BLOCK, pstride * p:pstride * (p + 1)] = jnp.zeros(
                        (ATT_KV_HEADS, BLOCK, pstride), BF16)

    for sub in range(refs[1].shape[1] // MIX_SUB):
        _mix_rows(sub, *refs, n_tiles=n_tiles)


def _mix_rows(sub, x_ref, q_ref, kl_ref, km_ref, kr_ref, vl_ref, vm_ref, vr_ref, ga_ref,
              qm_ref, gm_ref, mk_ref, mv_ref,
              xc_ref, zl_ref, rt_ref, rs_ref, sw_ref, tl_ref, gb_ref, zg_ref, dsk_ref, snw_ref, ef_ref, eb_ref,
              sink_ref, wo_ref, now_ref,
              o_ref, kst_ref, vst_ref, p_ref, att_ref, xat_ref, ssd_ref, hf_ref, *, n_tiles):
    i = pl.program_id(1)
    tq = MIX_SUB
    nsub = q_ref.shape[1] // tq
    nblk = tq // BLOCK
    rows = pl.ds(sub * tq, tq)

    if sub == 0:
        @pl.when(i == 0)
        def _():
            hf_ref[...] = jnp.zeros_like(hf_ref)
    for kk in range(tq // CHUNK):
        ck = sub * (tq // CHUNK) + kk
        crow = pl.ds(sub * tq + kk * CHUNK, CHUNK)
        _ssd_right(xc_ref[0, crow, :], zl_ref[crow, :], rt_ref[ck], rs_ref[crow, :], sw_ref[crow, :], tl_ref[ck],
                   gb_ref.at[ck], zg_ref[0, crow, :], dsk_ref, snw_ref, ef_ref, eb_ref, hf_ref, ssd_ref, crow)

    rr = lax.broadcasted_iota(jnp.int32, (BLOCK, WINDOW), 0)
    cc = lax.broadcasted_iota(jnp.int32, (BLOCK, WINDOW), 1)
    left_bias0 = jnp.where(cc >= rr, 0.0, NEG)
    right_bias0 = jnp.where(cc <= rr, 0.0, NEG)
    oslot = lax.broadcasted_iota(jnp.int32, (BLOCK, GROUP_W), 1) // ATT_HEAD_DIM
    pstride = 4 * BLOCK

    slot = lax.broadcasted_iota(jnp.int32, (BLOCK, GROUP_W), 1)
    kmask = [jnp.where(slot % LANES // HALF == j, 1.0, 0.0).astype(BF16) for j in range(4)]
    vmask = [jnp.where(slot // ATT_HEAD_DIM == j, 1.0, 0.0).astype(BF16) for j in range(4)]

    def piece(l_ref, m_ref, r_ref, p, lo):
        gp = sub * nblk + p
        if gp == 0:
            return l_ref[0, :, lo:lo + GROUP_W]
        if gp == nsub * nblk + 1:
            return r_ref[0, :, lo:lo + GROUP_W]
        return m_ref[0, (gp - 1) * BLOCK:gp * BLOCK, lo:lo + GROUP_W]

    def build_stacks(g):
        lo = GROUP_W * g
        for p in range(nblk + 2):
            kp = piece(kl_ref, km_ref, kr_ref, p, lo)
            vp = piece(vl_ref, vm_ref, vr_ref, p, lo)
            for j in range(4):
                rows = pl.ds(pstride * p + BLOCK * j, BLOCK)
                kst_ref[g, rows, :] = kp * kmask[j]
                vst_ref[g, rows, :] = vp * vmask[j]

    def scores(g):
        return _dot_nt(q_ref[0, rows, GROUP_W * g:GROUP_W * (g + 1)], kst_ref[g])

    def finish(g, s_all):
        lo = GROUP_W * g
        inv_rows = []
        for blk in range(nblk):
            r0 = blk * BLOCK
            left_dead = jnp.logical_and(i == 0, sub == 0 and blk == 0)
            right_dead = jnp.logical_and(i == n_tiles - 1, sub == nsub - 1 and blk == nblk - 1)
            left_bias = left_bias0 + jnp.where(left_dead, NEG, 0.0)
            right_bias = right_bias0 + jnp.where(right_dead, NEG, 0.0)
            inv = jnp.zeros((BLOCK, GROUP_W), F32)
            for j in range(4):
                c_l = pstride * blk + BLOCK * j
                c_m = c_l + pstride
                c_r = c_m + pstride
                s_l = s_all[r0:r0 + BLOCK, c_l:c_l + BLOCK] + left_bias
                s_m = s_all[r0:r0 + BLOCK, c_m:c_m + BLOCK]
                s_r = s_all[r0:r0 + BLOCK, c_r:c_r + BLOCK] + right_bias
                sink = sink_ref[0, 4 * g + j] * LOG2E
                m = jnp.maximum(jnp.max(jnp.maximum(jnp.maximum(s_l, s_m), s_r), axis=-1, keepdims=True), sink)
                p_l = jnp.exp2(s_l - m)
                p_m = jnp.exp2(s_m - m)
                p_r = jnp.exp2(s_r - m)
                den = jnp.sum(p_l + p_m + p_r, axis=-1, keepdims=True) + jnp.exp2(sink - m)
                p_ref[g, r0:r0 + BLOCK, c_l:c_l + BLOCK] = p_l.astype(BF16)
                p_ref[g, r0:r0 + BLOCK, c_m:c_m + BLOCK] = p_m.astype(BF16)
                p_ref[g, r0:r0 + BLOCK, c_r:c_r + BLOCK] = p_r.astype(BF16)
                inv = jnp.where(oslot == j, jnp.broadcast_to(1.0 / den, inv.shape), inv)
            inv_rows.append(inv)
        o = _dot(p_ref[g], vst_ref[g]) * jnp.concatenate(inv_rows, axis=0)
        gate = ga_ref[0, rows, lo:lo + GROUP_W].astype(F32)
        att_ref[rows, lo:lo + GROUP_W] = (o * gate).astype(BF16)

    build_stacks(0)
    s_next = scores(0)
    for g in range(ATT_KV_HEADS):
        s_cur = s_next
        if g + 1 < ATT_KV_HEADS:
            build_stacks(g + 1)
            s_next = scores(g + 1)
        finish(g, s_cur)

    for h in range(MEM_HEADS):
        lo = MEM_HEAD_DIM * h
        s = _dot_nt(qm_ref[0, rows, lo:lo + MEM_HEAD_DIM], mk_ref[0, :, lo:lo + MEM_HEAD_DIM])
        m = jnp.max(s, axis=-1, keepdims=True)
        p = jnp.exp2(s - m)
        den = jnp.sum(p, axis=-1, keepdims=True)
        o = _dot(p.astype(BF16), mv_ref[0, :, lo:lo + MEM_HEAD_DIM]) * (1.0 / den)
        xat_ref[rows, lo:lo + MEM_HEAD_DIM] = (o * gm_ref[0, rows, lo:lo + MEM_HEAD_DIM].astype(F32)).astype(BF16)

    delta = _dot(att_ref[rows, :], wo_ref[0:ATT_W, :])
    delta = delta + _dot(ssd_ref[rows, :], wo_ref[ATT_W:ATT_W + SSD_W, :])
    delta = delta + _dot(xat_ref[rows, :], wo_ref[ATT_W + SSD_W:, :])
    hres = x_ref[0, rows, :] + delta
    ms = jnp.mean(hres * hres, axis=-1, keepdims=True)
    o_ref[0, rows, :] = hres * lax.rsqrt(ms + EPS) * now_ref[...]


def _mix(x, q, k, v, ga, qm, gm, mk, mv, xc, zl, rt, rs, sw, tl, gb, zs, dskip, ssd_norm_w, e_f, e_b,
         sink, w_out_bf, norm_out_w):
    b, seq_len, _ = x.shape
    tq = MIX_TQ
    n_tiles = seq_len // tq
    bpt = tq // BLOCK
    sub_blocks = MIX_SUB // BLOCK
    n_blocks = seq_len // BLOCK
    span = BLOCK + 2 * WINDOW
    main = lambda w: pl.BlockSpec((1, tq, w), lambda bi, i: (bi, i, 0))
    left = lambda w: pl.BlockSpec((1, BLOCK, w), lambda bi, i: (bi, jnp.maximum(i * bpt - 1, 0), 0))
    right = lambda w: pl.BlockSpec((1, BLOCK, w), lambda bi, i: (bi, jnp.minimum((i + 1) * bpt, n_blocks - 1), 0))
    per_b = lambda r, w: pl.BlockSpec((1, r, w), lambda bi, i: (bi, 0, 0))
    const2 = lambda bi, i: (0, 0)
    cpt = tq // CHUNK
    tile_rows = pl.BlockSpec((tq, LANES), lambda bi, i: (bi * n_tiles + i, 0))
    per_chunk = lambda r, w: pl.BlockSpec((cpt, r, w), lambda bi, i: (bi * n_tiles + i, 0, 0))
    in_specs = [
        main(D_MODEL), main(ATT_W),
        left(ATT_W), main(ATT_W), right(ATT_W),
        left(ATT_W), main(ATT_W), right(ATT_W),
        main(ATT_W), main(MEM_W), main(MEM_W),
        per_b(MEM_LEN, MEM_W), per_b(MEM_LEN, MEM_W),
        main(XBC_W), tile_rows, per_chunk(N_DIR * SSD_HEADS, CHUNK), tile_rows, tile_rows,
        per_chunk(TL_ROWS, LANES),
        pl.BlockSpec((cpt, SSD_GROUPS, SSD_STATE, GROUP_W), lambda bi, i: (bi * n_tiles + i, 0, 0, 0)),
        main(SSD_W),
        pl.BlockSpec((1, SSD_W), const2), pl.BlockSpec((1, SSD_W), const2),
        pl.BlockSpec((LANES, SSD_W), const2), pl.BlockSpec((LANES, SSD_W), const2),
        pl.BlockSpec(memory_space=pltpu.SMEM),
        pl.BlockSpec((MIX_W, D_MODEL), const2),
        pl.BlockSpec((1, D_MODEL), const2),
    ]
    return pl.pallas_call(
        functools.partial(_mix_kernel, n_tiles=n_tiles),
        grid=(b, n_tiles),
        in_specs=in_specs,
        out_specs=pl.BlockSpec((1, tq, D_MODEL), lambda bi, i: (bi, i, 0)),
        out_shape=jax.ShapeDtypeStruct((b, seq_len, D_MODEL), F32),
        scratch_shapes=[
            pltpu.VMEM((ATT_KV_HEADS, (sub_blocks + 2) * 4 * BLOCK, GROUP_W), BF16),
            pltpu.VMEM((ATT_KV_HEADS, (sub_blocks + 2) * 4 * BLOCK, GROUP_W), BF16),
            pltpu.VMEM((ATT_KV_HEADS, MIX_SUB, (sub_blocks + 2) * 4 * BLOCK), BF16),
            pltpu.VMEM((tq, ATT_W), BF16),
            pltpu.VMEM((tq, MEM_W), BF16),
            pltpu.VMEM((tq, SSD_W), BF16),
            pltpu.VMEM((SSD_GROUPS, SSD_STATE, GROUP_W), F32),
        ],
        compiler_params=pltpu.CompilerParams(
            dimension_semantics=("arbitrary", "arbitrary"), vmem_limit_bytes=VMEM_LIMIT),
        name="mix",
    )(x, q, k, k, k, v, v, v, ga, qm, gm, mk, mv, xc, zl, rt, rs, sw, tl, gb, zs, dskip, ssd_norm_w, e_f, e_b,
      sink, w_out_bf, norm_out_w.reshape(1, D_MODEL))


def _q_perm():
    idx = np.zeros((ATT_W,), np.int32)
    for g in range(ATT_KV_HEADS):
        for half in range(2):
            for j in range(4):
                for d in range(HALF):
                    idx[GROUP_W * g + LANES * half + HALF * j + d] = (4 * g + j) * ATT_HEAD_DIM + HALF * half + d
    return idx


def _k_perm():
    idx = np.zeros((KV_W,), np.int32)
    for half in range(2):
        for g in range(ATT_KV_HEADS):
            for d in range(HALF):
                idx[LANES * half + HALF * g + d] = g * ATT_HEAD_DIM + HALF * half + d
    return idx


def kernel(x, mem, norm_mem_w, norm_in_w, w_in, attn_sink, conv_w, conv_b, dt_bias, a_log, d_skip,
           ssd_norm_w, w_mem_kv, w_out, norm_out_w):
    b, seq_len, _ = x.shape
    depth = w_in.shape[0]

    pos = jnp.arange(seq_len, dtype=F32)
    inv_freq = 1.0 / (ROPE_THETA ** (jnp.arange(0, ATT_HEAD_DIM, 2, dtype=F32) / ATT_HEAD_DIM))
    ang = pos[:, None] * inv_freq[None, :]
    cos128 = jnp.tile(jnp.cos(ang), (1, 4))
    sin128 = jnp.tile(jnp.sin(ang), (1, 4))

    assert depth == 1, "the final norm is fused into the single layer's output stage"
    layer = 0
    w = w_in[layer].astype(BF16)
    o = 0
    parts = []
    for width in (ATT_W, KV_W, KV_W, ATT_W, SSD_W, XBC_W, N_DIR * SSD_HEADS, MEM_W, MEM_W):
        parts.append(w[:, o:o + width])
        o += width
    wq, wk, wv, wga, wz, wxbc, wdt, wqm, wgm = parts
    wq = wq[:, _q_perm()]
    wk = wk[:, _k_perm()]
    pad16 = LANES - N_DIR * SSD_HEADS
    wdt = jnp.pad(wdt, ((0, 0), (0, pad16)))
    weights = [wq, wk, wv, wga, wz, wxbc, wdt, wqm, wgm]

    mk, mv = _memkv(mem, norm_mem_w, w_mem_kv[layer].astype(BF16))
    dt_bias128 = jnp.pad(dt_bias[layer].reshape(1, -1).astype(F32), ((0, 0), (0, pad16)))
    a128 = jnp.pad(-jnp.exp(a_log[layer].astype(F32)).reshape(1, -1), ((0, 0), (0, pad16)))
    dskip512 = jnp.repeat(d_skip[layer].astype(F32), SSD_HEAD_DIM).reshape(1, SSD_W)
    conv_w8 = jnp.pad(conv_w[layer].astype(F32), ((0, 8 - SSD_CONV), (0, 0)))
    e_f, e_b = _ssd_tables()
    q, k, v, ga, zs, qm, gm, xc, zl, rt, rs, sw, tl, gb = _proj(
        x.reshape(b * seq_len, D_MODEL), norm_in_w[layer], cos128, sin128, weights, conv_w8,
        conv_b[layer].reshape(1, XBC_W).astype(F32), dt_bias128, a128, e_b, seq_len)

    r3 = lambda arr: arr.reshape(b, seq_len, arr.shape[-1])
    return _mix(x, r3(q), r3(k), r3(v), r3(ga), r3(qm), r3(gm), mk, mv, r3(xc), zl, rt, rs, sw, tl, gb, r3(zs),
                dskip512, ssd_norm_w[layer].reshape(1, SSD_W).astype(F32), e_f, e_b,
                attn_sink[layer].reshape(1, ATT_HEADS).astype(F32), w_out[layer].astype(BF16), norm_out_w)
```

```python
import functools
import math

import jax
import jax.numpy as jnp
import numpy as np
from jax import lax
from jax.experimental import pallas as pl
from jax.experimental.pallas import tpu as pltpu

F32 = jnp.float32
BF16 = jnp.bfloat16

D_MODEL = 1024
ATT_HEADS = 16
ATT_KV_HEADS = 4
ATT_HEAD_DIM = 64
HALF = ATT_HEAD_DIM // 2
ATT_W = ATT_HEADS * ATT_HEAD_DIM
KV_W = ATT_KV_HEADS * ATT_HEAD_DIM
WINDOW = 128
BLOCK = 128
ROPE_THETA = 10000.0
SSD_HEADS = 8
SSD_HEAD_DIM = 64
SSD_W = SSD_HEADS * SSD_HEAD_DIM
SSD_GROUPS = 2
SSD_HPG = SSD_HEADS // SSD_GROUPS
SSD_STATE = 128
SSD_CONV = 5
CHUNK = 128
N_DIR = 2
XBC_W = SSD_W + 2 * SSD_GROUPS * SSD_STATE
MEM_LEN = 256
MEM_HEADS = 4
MEM_HEAD_DIM = 128
MEM_W = MEM_HEADS * MEM_HEAD_DIM
MIX_W = ATT_W + SSD_W + MEM_W
EPS = 1e-6

LANES = 128
GROUP_W = 256
HALO = 16
NEG = -1e30
VMEM_LIMIT = 56 * 1024 * 1024
LOG2E = math.log2(math.e)

PROJ_TM = 512
PROJ_SUB = 256
MIX_TQ = 512
MIX_SUB = 256
SSD_CPS = 4
UNITS_PER_SUB = 16


def _silu(v):
    return v * (1.0 / (1.0 + jnp.exp2(v * (-LOG2E))))


def _softplus(v):
    return jnp.maximum(v, 0.0) + jnp.log(1.0 + jnp.exp(-jnp.abs(v)))


def _dot(a, b):
    return jnp.dot(a, b, preferred_element_type=F32)


def _dot_nt(a, b):
    return lax.dot_general(a, b, (((1,), (1,)), ((), ())), preferred_element_type=F32)


def _split3(v):
    hi = v.astype(BF16)
    r1 = v - hi.astype(F32)
    mid = r1.astype(BF16)
    lo = (r1 - mid.astype(F32)).astype(BF16)
    return hi, mid, lo


def _memkv_kernel(mem_ref, nw_ref, w_ref, mk_ref, mv_ref):
    m = mem_ref[0]
    ms = jnp.mean(m * m, axis=-1, keepdims=True)
    mn = (m * lax.rsqrt(ms + EPS) * nw_ref[...]).astype(BF16)
    mk_ref[0] = _dot(mn, w_ref[:, :MEM_W]).astype(BF16)
    mv_ref[0] = _dot(mn, w_ref[:, MEM_W:]).astype(BF16)


def _memkv(mem, norm_mem_w, w_mem_kv_bf):
    b = mem.shape[0]
    return pl.pallas_call(
        _memkv_kernel,
        grid=(b,),
        in_specs=[
            pl.BlockSpec((1, MEM_LEN, D_MODEL), lambda i: (i, 0, 0)),
            pl.BlockSpec((1, D_MODEL), lambda i: (0, 0)),
            pl.BlockSpec((D_MODEL, 2 * MEM_W), lambda i: (0, 0)),
        ],
        out_specs=[
            pl.BlockSpec((1, MEM_LEN, MEM_W), lambda i: (i, 0, 0)),
            pl.BlockSpec((1, MEM_LEN, MEM_W), lambda i: (i, 0, 0)),
        ],
        out_shape=[jax.ShapeDtypeStruct((b, MEM_LEN, MEM_W), BF16)] * 2,
        compiler_params=pltpu.CompilerParams(
            dimension_semantics=("arbitrary",), vmem_limit_bytes=VMEM_LIMIT),
        name="memkv",
    )(mem, norm_mem_w.reshape(1, D_MODEL), w_mem_kv_bf)


TL_ROWS = HALO
N_PROJ_W = 9


def _ssd_expand(mats, tl_part, e):
    lhs = jnp.concatenate(list(mats) + list(_split3(tl_part)), axis=0)
    r = _dot(lhs, e)
    n = len(mats) * CHUNK
    outs = [r[CHUNK * m:CHUNK * (m + 1), :] for m in range(len(mats))]
    dec = jnp.exp2(r[n:n + TL_ROWS, :] + r[n + TL_ROWS:n + 2 * TL_ROWS, :] + r[n + 2 * TL_ROWS:n + 3 * TL_ROWS, :])
    return outs, dec


def _ssd_state_update(st_ref, xc, wexp, dec):
    for g in range(SSD_GROUPS):
        lo = GROUP_W * g
        xs = xc[:, lo:lo + GROUP_W]
        bm = xc[:, SSD_W + SSD_STATE * g:SSD_W + SSD_STATE * (g + 1)]
        xw = (xs * wexp[:, lo:lo + GROUP_W]).astype(BF16)
        upd = _dot(bm.T.astype(BF16), xw)
        st_ref[g] = st_ref[g] * dec[0:1, lo:lo + GROUP_W] + upd


def _proj_kernel(x_ref, xh_ref, nw_ref, cos_ref, sin_ref, *refs, tiles_per_seq):
    w_refs = refs[:N_PROJ_W]
    cw_ref, cb_ref, dtb_ref, a_ref, eb_ref = refs[N_PROJ_W:N_PROJ_W + 5]
    outs = refs[N_PROJ_W + 5:]
    q_out, k_out, v_out, ga_out, z_out, qm_out, gm_out = outs[:7]
    xc_out, zl_out, rt_out, rs_out, sw_out, tl_out, gb_out = outs[7:14]
    win_ref, dt_ref, gb_ref = outs[14:]

    tm = x_ref.shape[0]
    cps = tm // CHUNK
    j = tiles_per_seq - 1 - pl.program_id(0) % tiles_per_seq
    first_step = j == tiles_per_seq - 1
    main0 = HALO

    @pl.when(first_step)
    def _():
        gb_ref[...] = jnp.zeros_like(gb_ref)

    prev_head = win_ref[main0:main0 + HALO, :]
    win_ref[main0 + tm:main0 + tm + HALO, :] = jnp.where(first_step, jnp.zeros_like(prev_head), prev_head)

    subs = list(range(0, tm, PROJ_SUB))
    hns = {}
    for r0 in subs:
        rows = pl.ds(r0, PROJ_SUB)
        halo = (xh_ref, j == 0) if r0 == 0 else None
        hns[r0] = _proj_conv_cols(x_ref.at[rows], halo, nw_ref, w_refs[5], w_refs[6], win_ref, main0 + r0,
                                  dt_ref.at[rows])

    def rest_units():
        for r0 in subs:
            rows = pl.ds(r0, PROJ_SUB)
            yield from _proj_rows(hns[r0], cos_ref.at[rows], sin_ref.at[rows], *w_refs,
                                  *[o.at[rows] for o in (q_out, k_out, v_out, ga_out, z_out, qm_out, gm_out)])

    def decay_tables(k):
        rr = lax.broadcasted_iota(jnp.int32, (CHUNK, CHUNK), 0)
        cc = lax.broadcasted_iota(jnp.int32, (CHUNK, CHUNK), 1)
        fwd_col = cc < SSD_HEADS
        dt = _softplus(dt_ref[CHUNK * k:CHUNK * (k + 1), :] + dtb_ref[...])
        adt = dt * (a_ref[...] * LOG2E)
        cum = jnp.where(rr >= cc, 1.0, 0.0).astype(BF16)
        cum = jnp.concatenate([cum, jnp.ones((CHUNK, CHUNK), BF16)], axis=0)
        r3 = _dot(cum, jnp.concatenate(_split3(adt), axis=1))
        r3 = r3[:, 0:LANES] + r3[:, LANES:2 * LANES] + r3[:, 2 * LANES:]
        cs = r3[0:CHUNK, :]
        tl = r3[CHUNK:, :]
        zl = jnp.where(fwd_col, cs, cs - adt)
        ldt = jnp.log2(dt)
        rt = jnp.where(fwd_col, zl - ldt, zl + ldt).T
        rs = jnp.exp2(jnp.where(fwd_col, zl, tl - zl)).astype(BF16)
        sw = (jnp.exp2(jnp.where(fwd_col, tl - zl, zl)) * dt).astype(BF16)
        zl_out[CHUNK * k:CHUNK * (k + 1), :] = zl
        rt_out[k] = rt[0:N_DIR * SSD_HEADS, :]
        rs_out[CHUNK * k:CHUNK * (k + 1), :] = rs
        sw_out[CHUNK * k:CHUNK * (k + 1), :] = sw
        tl_out[k] = tl[0:TL_ROWS, :]
        return sw, tl[0:TL_ROWS, :]

    def conv_silu(k):
        pad = (SSD_CONV - 1) // 2
        acc = cb_ref[...]
        for jj in range(SSD_CONV):
            acc = acc + cw_ref[jj:jj + 1, :] * win_ref[pl.ds(main0 + CHUNK * k - pad + jj, CHUNK), :]
        xc = _silu(acc)
        xc_out[CHUNK * k:CHUNK * (k + 1), :] = xc.astype(BF16)
        return xc

    order = list(reversed(range(cps)))
    tables, xcs = {}, {}

    def prepare(k):
        tables[k] = decay_tables(k)
        xcs[k] = conv_silu(k)

    def advance(k):
        sw, tl_part = tables[k]
        gb_out[k] = gb_ref[...].astype(BF16)
        (wexp,), dec = _ssd_expand([sw], tl_part, eb_ref[...])
        _ssd_state_update(gb_ref, xcs[k], wexp, dec)

    pieces = [functools.partial(prepare, k) for k in order] + [functools.partial(advance, k) for k in order]
    units = list(range(UNITS_PER_SUB * len(subs)))
    every = max(len(units) // (len(pieces) + 1), 1)
    stream = rest_units()
    for u in units:
        next(stream)
        if (u + 1) % every == 0 and pieces:
            pieces.pop(0)()
    for piece in pieces:
        piece()


def _proj_conv_cols(x_ref, halo, nw_ref, wxbc_ref, wdt_ref, win_ref, win_row, dt_out):
    def normed(rows_ref):
        xr = rows_ref[...]
        ms = jnp.mean(xr * xr, axis=-1, keepdims=True)
        return (xr * lax.rsqrt(ms + EPS) * nw_ref[...]).astype(BF16)

    hn = normed(x_ref)
    n_rows = hn.shape[0]
    if halo is None:
        lhs = hn
    else:
        xh_ref, at_seq_start = halo
        lhs = jnp.concatenate([normed(xh_ref), hn], axis=0)
    for j in range(XBC_W // GROUP_W):
        t = _dot(lhs, wxbc_ref[:, GROUP_W * j:GROUP_W * (j + 1)])
        cols = slice(GROUP_W * j, GROUP_W * (j + 1))
        if halo is not None:
            left = t[0:HALO, :]
            win_ref[win_row - HALO:win_row, cols] = jnp.where(at_seq_start, jnp.zeros_like(left), left)
            t = t[HALO:, :]
        win_ref[win_row:win_row + n_rows, cols] = t
    dt_out[...] = _dot(hn, wdt_ref[...])
    return hn


def _proj_rows(hn, cos_ref, sin_ref,
               wq_ref, wk_ref, wv_ref, wga_ref, wz_ref, wxbc_ref, wdt_ref, wqm_ref, wgm_ref,
               q_out, k_out, v_out, ga_out, z_out, qm_out, gm_out):
    c = cos_ref[...]
    s = sin_ref[...]
    att_scale = ATT_HEAD_DIM ** -0.5 * LOG2E
    mem_scale = MEM_HEAD_DIM ** -0.5 * LOG2E

    for g in range(ATT_KV_HEADS):
        t = _dot(hn, wq_ref[:, GROUP_W * g:GROUP_W * (g + 1)])
        t1 = t[:, :LANES]
        t2 = t[:, LANES:]
        q_out[:, GROUP_W * g:GROUP_W * g + LANES] = ((t1 * c - t2 * s) * att_scale).astype(BF16)
        q_out[:, GROUP_W * g + LANES:GROUP_W * (g + 1)] = ((t1 * s + t2 * c) * att_scale).astype(BF16)
        yield

    t = _dot(hn, wk_ref[...])
    t1 = t[:, :LANES]
    t2 = t[:, LANES:]
    ko = (t1 * c - t2 * s, t1 * s + t2 * c)
    slot32 = lax.broadcasted_iota(jnp.int32, t1.shape, 1) // HALF
    for half in range(2):
        r = [ko[half]] + [pltpu.roll(ko[half], HALF * m, 1) for m in range(1, 4)]
        for g in range(ATT_KV_HEADS):
            rep = r[(3 - g) % 4]
            for j in (2, 1, 0):
                rep = jnp.where(slot32 == j, r[(j - g) % 4], rep)
            k_out[:, GROUP_W * g + LANES * half:GROUP_W * g + LANES * (half + 1)] = rep.astype(BF16)
    yield

    t = _dot(hn, wv_ref[...])
    low = lax.broadcasted_iota(jnp.int32, (t.shape[0], LANES), 1) < ATT_HEAD_DIM
    for pair in range(2):
        a = t[:, LANES * pair:LANES * (pair + 1)]
        ra = pltpu.roll(a, ATT_HEAD_DIM, 1)
        even = jnp.where(low, a, ra).astype(BF16)
        odd = jnp.where(low, ra, a).astype(BF16)
        for rep in range(2):
            v_out[:, GROUP_W * (2 * pair) + LANES * rep:GROUP_W * (2 * pair) + LANES * (rep + 1)] = even
            v_out[:, GROUP_W * (2 * pair + 1) + LANES * rep:GROUP_W * (2 * pair + 1) + LANES * (rep + 1)] = odd
    yield

    for j in range(ATT_W // GROUP_W):
        t = _dot(hn, wga_ref[:, GROUP_W * j:GROUP_W * (j + 1)])
        ga_out[:, GROUP_W * j:GROUP_W * (j + 1)] = _silu(t).astype(BF16)
        yield
    for j in range(SSD_W // GROUP_W):
        t = _dot(hn, wz_ref[:, GROUP_W * j:GROUP_W * (j + 1)])
        z_out[:, GROUP_W * j:GROUP_W * (j + 1)] = _silu(t).astype(BF16)
        yield
    for j in range(MEM_W // GROUP_W):
        t = _dot(hn, wqm_ref[:, GROUP_W * j:GROUP_W * (j + 1)])
        qm_out[:, GROUP_W * j:GROUP_W * (j + 1)] = (t * mem_scale).astype(BF16)
        yield
        t = _dot(hn, wgm_ref[:, GROUP_W * j:GROUP_W * (j + 1)])
        gm_out[:, GROUP_W * j:GROUP_W * (j + 1)] = _silu(t).astype(BF16)
        yield


def _ssd_right(xc_bf, zl, rt, rs, sw, tl_part, gb_next, zgate, dsk_ref, nw_ref, ef_ref, eb_ref, hf_ref, y_ref, rows):
    rr = lax.broadcasted_iota(jnp.int32, (CHUNK, CHUNK), 0)
    cc = lax.broadcasted_iota(jnp.int32, (CHUNK, CHUNK), 1)
    lower = rr >= cc
    upper = rr <= cc
    slot = lax.broadcasted_iota(jnp.int32, (CHUNK, GROUP_W), 1) // SSD_HEAD_DIM
    slotmask = [jnp.where(slot == j, 1.0, 0.0).astype(BF16) for j in range(SSD_HPG)]

    xc = xc_bf.astype(F32)
    scale_f = _dot(rs, ef_ref[...])
    scale_b = _dot(rs, eb_ref[...])
    for g in range(SSD_GROUPS):
        lo = GROUP_W * g
        xs = xc[:, lo:lo + GROUP_W]
        bm = xc_bf[:, SSD_W + SSD_STATE * g:SSD_W + SSD_STATE * (g + 1)]
        cm = xc_bf[:, SSD_W + SSD_GROUPS * SSD_STATE + SSD_STATE * g:
                   SSD_W + SSD_GROUPS * SSD_STATE + SSD_STATE * (g + 1)]
        gmat = _dot_nt(cm, bm)
        xs_bf = xc_bf[:, lo:lo + GROUP_W]
        m_parts = []
        x_parts = []
        for j in range(SSD_HPG):
            hf = SSD_HPG * g + j
            hb = SSD_HEADS + hf
            w_f = jnp.where(lower, jnp.exp2(zl[:, hf:hf + 1] - rt[hf:hf + 1, :]), 0.0)
            w_b = jnp.where(upper, jnp.exp2(rt[hb:hb + 1, :] - zl[:, hb:hb + 1]), 0.0)
            m_parts.append((gmat * (w_f + w_b)).astype(BF16))
            x_parts.append(xs_bf * slotmask[j])
        m_cat = jnp.concatenate(m_parts, axis=1)
        x_stack = jnp.concatenate(x_parts, axis=0)
        y = _dot(m_cat, x_stack)
        y = y + _dot(cm, hf_ref[g].astype(BF16)) * scale_f[:, lo:lo + GROUP_W]
        y = y + _dot(cm, gb_next[g]) * scale_b[:, lo:lo + GROUP_W]
        y = y + dsk_ref[:, lo:lo + GROUP_W] * xs
        gated = y * zgate[:, lo:lo + GROUP_W].astype(F32)
        ms = jnp.mean(gated * gated, axis=-1, keepdims=True)
        y_ref[rows, lo:lo + GROUP_W] = (gated * lax.rsqrt(ms + EPS) * nw_ref[:, lo:lo + GROUP_W]).astype(BF16)
    (wexp,), dec = _ssd_expand([sw], tl_part, ef_ref[...])
    _ssd_state_update(hf_ref, xc, wexp, dec)


def _ssd_tables():
    hcol = np.arange(LANES)[:, None]
    hslot = (np.arange(SSD_W) // SSD_HEAD_DIM)[None, :]
    e_f = (hcol == hslot).astype(np.float32)
    e_b = (hcol == hslot + SSD_HEADS).astype(np.float32)
    return jnp.asarray(e_f, BF16), jnp.asarray(e_b, BF16)


def _proj(x2, norm_w, cos128, sin128, weights, conv_w8, conv_b, dt_bias128, a128, e_b, seq_len):
    t_rows = x2.shape[0]
    tm = PROJ_TM
    assert seq_len % tm == 0
    tiles_per_seq = seq_len // tm
    cps = tm // CHUNK
    halo_per_tile = tm // HALO
    win_rows = tm + 2 * HALO

    tile_of = lambda s: (s // tiles_per_seq) * tiles_per_seq + tiles_per_seq - 1 - s % tiles_per_seq
    row = lambda s: (tile_of(s), 0)
    row3 = lambda s: (tile_of(s), 0, 0)
    row4 = lambda s: (tile_of(s), 0, 0, 0)
    pos = lambda s: (tiles_per_seq - 1 - s % tiles_per_seq, 0)
    const = lambda s: (0, 0)
    resident = lambda a: pl.BlockSpec(a.shape, const, pipeline_mode=pl.Buffered(1))
    in_specs = [
        pl.BlockSpec((tm, D_MODEL), row),
        pl.BlockSpec((HALO, D_MODEL), lambda s: (jnp.maximum(tile_of(s) * halo_per_tile - 1, 0), 0)),
        pl.BlockSpec((1, D_MODEL), const),
        pl.BlockSpec((tm, LANES), pos),
        pl.BlockSpec((tm, LANES), pos),
    ] + [resident(w) for w in weights] + [
        pl.BlockSpec((8, XBC_W), const),
        pl.BlockSpec((1, XBC_W), const),
        pl.BlockSpec((1, LANES), const),
        pl.BlockSpec((1, LANES), const),
        resident(e_b),
    ]
    n_chunks = t_rows // CHUNK
    out_specs = [pl.BlockSpec((tm, w), row) for w in (ATT_W, ATT_W, ATT_W, ATT_W, SSD_W, MEM_W, MEM_W)] + [
        pl.BlockSpec((tm, XBC_W), row),
        pl.BlockSpec((tm, LANES), row),
        pl.BlockSpec((cps, N_DIR * SSD_HEADS, CHUNK), row3),
        pl.BlockSpec((tm, LANES), row),
        pl.BlockSpec((tm, LANES), row),
        pl.BlockSpec((cps, TL_ROWS, LANES), row3),
        pl.BlockSpec((cps, SSD_GROUPS, SSD_STATE, GROUP_W), row4),
    ]
    out_shape = [jax.ShapeDtypeStruct((t_rows, w), BF16) for w in (ATT_W, ATT_W, ATT_W, ATT_W, SSD_W, MEM_W, MEM_W)] + [
        jax.ShapeDtypeStruct((t_rows, XBC_W), BF16),
        jax.ShapeDtypeStruct((t_rows, LANES), F32),
        jax.ShapeDtypeStruct((n_chunks, N_DIR * SSD_HEADS, CHUNK), F32),
        jax.ShapeDtypeStruct((t_rows, LANES), BF16),
        jax.ShapeDtypeStruct((t_rows, LANES), BF16),
        jax.ShapeDtypeStruct((n_chunks, TL_ROWS, LANES), F32),
        jax.ShapeDtypeStruct((n_chunks, SSD_GROUPS, SSD_STATE, GROUP_W), BF16),
    ]
    return pl.pallas_call(
        functools.partial(_proj_kernel, tiles_per_seq=tiles_per_seq),
        grid=(t_rows // tm,),
        in_specs=in_specs,
        out_specs=out_specs,
        out_shape=out_shape,
        scratch_shapes=[
            pltpu.VMEM((win_rows, XBC_W), F32),
            pltpu.VMEM((tm, LANES), F32),
            pltpu.VMEM((SSD_GROUPS, SSD_STATE, GROUP_W), F32),
        ],
        compiler_params=pltpu.CompilerParams(
            dimension_semantics=("arbitrary",), vmem_limit_bytes=VMEM_LIMIT),
        name="proj",
    )(x2, x2, norm_w.reshape(1, D_MODEL), cos128, sin128, *weights, conv_w8, conv_b, dt_bias128, a128, e_b)


def _mix_kernel(*refs, n_tiles):
    p_ref = refs[-5]
    nblk = MIX_SUB // BLOCK
    pstride = 4 * BLOCK

    @pl.when(jnp.logical_and(pl.program_id(0) == 0, pl.program_id(1) == 0))
    def _():
        for blk in range(nblk):
            for p in range(nblk + 2):
                if p < blk or p > blk + 2:
                    p_ref[:, blk * BLOCK:(blk + 1) * BLOCK, pstride * p:pstride * (p + 1)] = jnp.zeros(
                        (ATT_KV_HEADS, BLOCK, pstride), BF16)

    for sub in range(refs[1].shape[1] // MIX_SUB):
        _mix_rows(sub, *refs, n_tiles=n_tiles)


def _mix_rows(sub, x_ref, q_ref, kl_ref, km_ref, kr_ref, vl_ref, vm_ref, vr_ref, ga_ref,
              qm_ref, gm_ref, mk_ref, mv_ref,
              xc_ref, zl_ref, rt_ref, rs_ref, sw_ref, tl_ref, gb_ref, zg_ref, dsk_ref, snw_ref, ef_ref, eb_ref,
              sink_ref, wo_ref, now_ref,
              o_ref, kst_ref, vst_ref, p_ref, att_ref, xat_ref, ssd_ref, hf_ref, *, n_tiles):
    i = pl.program_id(1)
    tq = MIX_SUB
    nsub = q_ref.shape[1] // tq
    nblk = tq // BLOCK
    rows = pl.ds(sub * tq, tq)

    if sub == 0:
        @pl.when(i == 0)
        def _():
            hf_ref[...] = jnp.zeros_like(hf_ref)
    for kk in range(tq // CHUNK):
        ck = sub * (tq // CHUNK) + kk
        crow = pl.ds(sub * tq + kk * CHUNK, CHUNK)
        _ssd_right(xc_ref[0, crow, :], zl_ref[crow, :], rt_ref[ck], rs_ref[crow, :], sw_ref[crow, :], tl_ref[ck],
                   gb_ref.at[ck], zg_ref[0, crow, :], dsk_ref, snw_ref, ef_ref, eb_ref, hf_ref, ssd_ref, crow)

    rr = lax.broadcasted_iota(jnp.int32, (BLOCK, WINDOW), 0)
    cc = lax.broadcasted_iota(jnp.int32, (BLOCK, WINDOW), 1)
    left_bias0 = jnp.where(cc >= rr, 0.0, NEG)
    right_bias0 = jnp.where(cc <= rr, 0.0, NEG)
    oslot = lax.broadcasted_iota(jnp.int32, (BLOCK, GROUP_W), 1) // ATT_HEAD_DIM
    pstride = 4 * BLOCK

    slot = lax.broadcasted_iota(jnp.int32, (BLOCK, GROUP_W), 1)
    kmask = [jnp.where(slot % LANES // HALF == j, 1.0, 0.0).astype(BF16) for j in range(4)]
    vmask = [jnp.where(slot // ATT_HEAD_DIM == j, 1.0, 0.0).astype(BF16) for j in range(4)]

    def piece(l_ref, m_ref, r_ref, p, lo):
        gp = sub * nblk + p
        if gp == 0:
            return l_ref[0, :, lo:lo + GROUP_W]
        if gp == nsub * nblk + 1:
            return r_ref[0, :, lo:lo + GROUP_W]
        return m_ref[0, (gp - 1) * BLOCK:gp * BLOCK, lo:lo + GROUP_W]

    def build_stacks(g):
        lo = GROUP_W * g
        for p in range(nblk + 2):
            kp = piece(kl_ref, km_ref, kr_ref, p, lo)
            vp = piece(vl_ref, vm_ref, vr_ref, p, lo)
            for j in range(4):
                rows = pl.ds(pstride * p + BLOCK * j, BLOCK)
                kst_ref[g, rows, :] = kp * kmask[j]
                vst_ref[g, rows, :] = vp * vmask[j]

    def scores(g):
        return _dot_nt(q_ref[0, rows, GROUP_W * g:GROUP_W * (g + 1)], kst_ref[g])

    def finish(g, s_all):
        lo = GROUP_W * g
        inv_rows = []
        for blk in range(nblk):
            r0 = blk * BLOCK
            left_dead = jnp.logical_and(i == 0, sub == 0 and blk == 0)
            right_dead = jnp.logical_and(i == n_tiles - 1, sub == nsub - 1 and blk == nblk - 1)
            left_bias = left_bias0 + jnp.where(left_dead, NEG, 0.0)
            right_bias = right_bias0 + jnp.where(right_dead, NEG, 0.0)
            inv = jnp.zeros((BLOCK, GROUP_W), F32)
            for j in range(4):
                c_l = pstride * blk + BLOCK * j
                c_m = c_l + pstride
                c_r = c_m + pstride
                s_l = s_all[r0:r0 + BLOCK, c_l:c_l + BLOCK] + left_bias
                s_m = s_all[r0:r0 + BLOCK, c_m:c_m + BLOCK]
                s_r = s_all[r0:r0 + BLOCK, c_r:c_r + BLOCK] + right_bias
                sink = sink_ref[0, 4 * g + j] * LOG2E
                m = jnp.maximum(jnp.max(jnp.maximum(jnp.maximum(s_l, s_m), s_r), axis=-1, keepdims=True), sink)
                p_l = jnp.exp2(s_l - m)
                p_m = jnp.exp2(s_m - m)
                p_r = jnp.exp2(s_r - m)
                den = jnp.sum(p_l + p_m + p_r, axis=-1, keepdims=True) + jnp.exp2(sink - m)
                p_ref[g, r0:r0 + BLOCK, c_l:c_l + BLOCK] = p_l.astype(BF16)
                p_ref[g, r0:r0 + BLOCK, c_m:c_m + BLOCK] = p_m.astype(BF16)
                p_ref[g, r0:r0 + BLOCK, c_r:c_r + BLOCK] = p_r.astype(BF16)
                inv = jnp.where(oslot == j, jnp.broadcast_to(1.0 / den, inv.shape), inv)
            inv_rows.append(inv)
        o = _dot(p_ref[g], vst_ref[g]) * jnp.concatenate(inv_rows, axis=0)
        gate = ga_ref[0, rows, lo:lo + GROUP_W].astype(F32)
        att_ref[rows, lo:lo + GROUP_W] = (o * gate).astype(BF16)

    build_stacks(0)
    s_next = scores(0)
    for g in range(ATT_KV_HEADS):
        s_cur = s_next
        if g + 1 < ATT_KV_HEADS:
            build_stacks(g + 1)
            s_next = scores(g + 1)
        finish(g, s_cur)

    for h in range(MEM_HEADS):
        lo = MEM_HEAD_DIM * h
        s = _dot_nt(qm_ref[0, rows, lo:lo + MEM_HEAD_DIM], mk_ref[0, :, lo:lo + MEM_HEAD_DIM])
        m = jnp.max(s, axis=-1, keepdims=True)
        p = jnp.exp2(s - m)
        den = jnp.sum(p, axis=-1, keepdims=True)
        o = _dot(p.astype(BF16), mv_ref[0, :, lo:lo + MEM_HEAD_DIM]) * (1.0 / den)
        xat_ref[rows, lo:lo + MEM_HEAD_DIM] = (o * gm_ref[0, rows, lo:lo + MEM_HEAD_DIM].astype(F32)).astype(BF16)

    delta = _dot(att_ref[rows, :], wo_ref[0:ATT_W, :])
    delta = delta + _dot(ssd_ref[rows, :], wo_ref[ATT_W:ATT_W + SSD_W, :])
    delta = delta + _dot(xat_ref[rows, :], wo_ref[ATT_W + SSD_W:, :])
    hres = x_ref[0, rows, :] + delta
    ms = jnp.mean(hres * hres, axis=-1, keepdims=True)
    o_ref[0, rows, :] = hres * lax.rsqrt(ms + EPS) * now_ref[...]


def _mix(x, q, k, v, ga, qm, gm, mk, mv, xc, zl, rt, rs, sw, tl, gb, zs, dskip, ssd_norm_w, e_f, e_b,
         sink, w_out_bf, norm_out_w):
    b, seq_len, _ = x.shape
    tq = MIX_TQ
    n_tiles = seq_len // tq
    bpt = tq // BLOCK
    sub_blocks = MIX_SUB // BLOCK
    n_blocks = seq_len // BLOCK
    span = BLOCK + 2 * WINDOW
    main = lambda w: pl.BlockSpec((1, tq, w), lambda bi, i: (bi, i, 0))
    left = lambda w: pl.BlockSpec((1, BLOCK, w), lambda bi, i: (bi, jnp.maximum(i * bpt - 1, 0), 0))
    right = lambda w: pl.BlockSpec((1, BLOCK, w), lambda bi, i: (bi, jnp.minimum((i + 1) * bpt, n_blocks - 1), 0))
    per_b = lambda r, w: pl.BlockSpec((1, r, w), lambda bi, i: (bi, 0, 0))
    const2 = lambda bi, i: (0, 0)
    cpt = tq // CHUNK
    tile_rows = pl.BlockSpec((tq, LANES), lambda bi, i: (bi * n_tiles + i, 0))
    per_chunk = lambda r, w: pl.BlockSpec((cpt, r, w), lambda bi, i: (bi * n_tiles + i, 0, 0))
    in_specs = [
        main(D_MODEL), main(ATT_W),
        left(ATT_W), main(ATT_W), right(ATT_W),
        left(ATT_W), main(ATT_W), right(ATT_W),
        main(ATT_W), main(MEM_W), main(MEM_W),
        per_b(MEM_LEN, MEM_W), per_b(MEM_LEN, MEM_W),
        main(XBC_W), tile_rows, per_chunk(N_DIR * SSD_HEADS, CHUNK), tile_rows, tile_rows,
        per_chunk(TL_ROWS, LANES),
        pl.BlockSpec((cpt, SSD_GROUPS, SSD_STATE, GROUP_W), lambda bi, i: (bi * n_tiles + i, 0, 0, 0)),
        main(SSD_W),
        pl.BlockSpec((1, SSD_W), const2), pl.BlockSpec((1, SSD_W), const2),
        pl.BlockSpec((LANES, SSD_W), const2), pl.BlockSpec((LANES, SSD_W), const2),
        pl.BlockSpec(memory_space=pltpu.SMEM),
        pl.BlockSpec((MIX_W, D_MODEL), const2),
        pl.BlockSpec((1, D_MODEL), const2),
    ]
    return pl.pallas_call(
        functools.partial(_mix_kernel, n_tiles=n_tiles),
        grid=(b, n_tiles),
        in_specs=in_specs,
        out_specs=pl.BlockSpec((1, tq, D_MODEL), lambda bi, i: (bi, i, 0)),
        out_shape=jax.ShapeDtypeStruct((b, seq_len, D_MODEL), F32),
        scratch_shapes=[
            pltpu.VMEM((ATT_KV_HEADS, (sub_blocks + 2) * 4 * BLOCK, GROUP_W), BF16),
            pltpu.VMEM((ATT_KV_HEADS, (sub_blocks + 2) * 4 * BLOCK, GROUP_W), BF16),
            pltpu.VMEM((ATT_KV_HEADS, MIX_SUB, (sub_blocks + 2) * 4 * BLOCK), BF16),
            pltpu.VMEM((tq, ATT_W), BF16),
            pltpu.VMEM((tq, MEM_W), BF16),
            pltpu.VMEM((tq, SSD_W), BF16),
            pltpu.VMEM((SSD_GROUPS, SSD_STATE, GROUP_W), F32),
        ],
        compiler_params=pltpu.CompilerParams(
            dimension_semantics=("arbitrary", "arbitrary"), vmem_limit_bytes=VMEM_LIMIT),
        name="mix",
    )(x, q, k, k, k, v, v, v, ga, qm, gm, mk, mv, xc, zl, rt, rs, sw, tl, gb, zs, dskip, ssd_norm_w, e_f, e_b,
      sink, w_out_bf, norm_out_w.reshape(1, D_MODEL))


def _q_perm():
    idx = np.zeros((ATT_W,), np.int32)
    for g in range(ATT_KV_HEADS):
        for half in range(2):
            for j in range(4):
                for d in range(HALF):
                    idx[GROUP_W * g + LANES * half + HALF * j + d] = (4 * g + j) * ATT_HEAD_DIM + HALF * half + d
    return idx


def _k_perm():
    idx = np.zeros((KV_W,), np.int32)
    for half in range(2):
        for g in range(ATT_KV_HEADS):
            for d in range(HALF):
                idx[LANES * half + HALF * g + d] = g * ATT_HEAD_DIM + HALF * half + d
    return idx


def kernel(x, mem, norm_mem_w, norm_in_w, w_in, attn_sink, conv_w, conv_b, dt_bias, a_log, d_skip,
           ssd_norm_w, w_mem_kv, w_out, norm_out_w):
    b, seq_len, _ = x.shape
    depth = w_in.shape[0]

    pos = jnp.arange(seq_len, dtype=F32)
    inv_freq = 1.0 / (ROPE_THETA ** (jnp.arange(0, ATT_HEAD_DIM, 2, dtype=F32) / ATT_HEAD_DIM))
    ang = pos[:, None] * inv_freq[None, :]
    cos128 = jnp.tile(jnp.cos(ang), (1, 4))
    sin128 = jnp.tile(jnp.sin(ang), (1, 4))

    assert depth == 1, "the final norm is fused into the single layer's output stage"
    layer = 0
    w = w_in[layer].astype(BF16)
    o = 0
    parts = []
    for width in (ATT_W, KV_W, KV_W, ATT_W, SSD_W, XBC_W, N_DIR * SSD_HEADS, MEM_W, MEM_W):
        parts.append(w[:, o:o + width])
        o += width
    wq, wk, wv, wga, wz, wxbc, wdt, wqm, wgm = parts
    wq = wq[:, _q_perm()]
    wk = wk[:, _k_perm()]
    pad16 = LANES - N_DIR * SSD_HEADS
    wdt = jnp.pad(wdt, ((0, 0), (0, pad16)))
    weights = [wq, wk, wv, wga, wz, wxbc, wdt, wqm, wgm]

    mk, mv = _memkv(mem, norm_mem_w, w_mem_kv[layer].astype(BF16))
    dt_bias128 = jnp.pad(dt_bias[layer].reshape(1, -1).astype(F32), ((0, 0), (0, pad16)))
    a128 = jnp.pad(-jnp.exp(a_log[layer].astype(F32)).reshape(1, -1), ((0, 0), (0, pad16)))
    dskip512 = jnp.repeat(d_skip[layer].astype(F32), SSD_HEAD_DIM).reshape(1, SSD_W)
    conv_w8 = jnp.pad(conv_w[layer].astype(F32), ((0, 8 - SSD_CONV), (0, 0)))
    e_f, e_b = _ssd_tables()
    q, k, v, ga, zs, qm, gm, xc, zl, rt, rs, sw, tl, gb = _proj(
        x.reshape(b * seq_len, D_MODEL), norm_in_w[layer], cos128, sin128, weights, conv_w8,
        conv_b[layer].reshape(1, XBC_W).astype(F32), dt_bias128, a128, e_b, seq_len)

    r3 = lambda arr: arr.reshape(b, seq_len, arr.shape[-1])
    return _mix(x, r3(q), r3(k), r3(v), r3(ga), r3(qm), r3(gm), mk, mv, r3(xc), zl, rt, rs, sw, tl, gb, r3(zs),
                dskip512, ssd_norm_w[layer].reshape(1, SSD_W).astype(F32), e_f, e_b,
                attn_sink[layer].reshape(1, ATT_HEADS).astype(F32), w_out[layer].astype(BF16), norm_out_w)
```
